```python
import math
import jax, jax.numpy as jnp
from jax import lax
import numpy as np

D_MODEL = 1024
BATCH = 4
SEQ = 4096
DEPTH = 1
DEC_BATCH = 2
DEC_SEQ = 16384
PAST_LEN = 128

HEAD_DIM = 64
DIL_GROUPS = ((128, 1), (512, 4), (2048, 16))
A_HEADS_PER_GROUP = 4
A_HEADS = A_HEADS_PER_GROUP * len(DIL_GROUPS)
A_WIDTH = A_HEADS * HEAD_DIM
A_OUT_WIDTH = A_HEADS_PER_GROUP * HEAD_DIM
T5_BUCKETS = 32
T5_MAX_DIST = 1024
GRID_W = 64
B_HEADS = 8
B_WIDTH = B_HEADS * HEAD_DIM
NB_ROWS = 8
NB_COLS = 16
NB_Q_COLS = 16
NB_K_COLS = 32
IN_WIDTH = 3 * A_WIDTH + 3 * B_WIDTH + 2 * D_MODEL
PEER_HEADS = 8
PEER_NKEYS = 128
PEER_EXPERTS = PEER_NKEYS * PEER_NKEYS
PEER_QDIM = 256
PEER_HALF = PEER_QDIM // 2
PEER_TOPK = 16
PEER_CHUNK = 128
NORM_EPS = 1e-6
NEG_INF = -1e30

kernel_name = "hybrid_dilated_neighbourhood_peer_encoder"


def rmsnorm(x, g):
    xf = x.astype(jnp.float32)
    y = xf * lax.rsqrt(jnp.mean(xf * xf, axis=-1, keepdims=True) + NORM_EPS)
    return (y * g.astype(jnp.float32)).astype(x.dtype)


def t5_bucket(rel):
    n = -rel
    nb = T5_BUCKETS // 2
    ret = (n < 0).astype(np.int32) * nb
    n = np.abs(n)
    max_exact = nb // 2
    large = max_exact + (np.log(np.maximum(n, 1) / max_exact) / math.log(T5_MAX_DIST / max_exact) * (nb - max_exact)).astype(np.int32)
    large = np.minimum(large, nb - 1)
    return (ret + np.where(n < max_exact, n, large)).astype(np.int32)


def dilated_group(q, k, v, bias_tab, window, dilation):
    bn, s_len, nh, hd = q.shape
    rad = window // (2 * dilation)
    sub_len = s_len // dilation
    nblk = -(-sub_len // rad)
    lp = nblk * rad

    def to_sub(t):
        t = t.reshape(bn, sub_len, dilation, nh, hd).transpose(0, 3, 2, 1, 4)
        return jnp.pad(t, ((0, 0), (0, 0), (0, 0), (0, lp - sub_len), (0, 0)))

    def windows(t):
        tp = jnp.pad(t, ((0, 0), (0, 0), (0, 0), (rad, rad), (0, 0)))
        parts = [tp[:, :, :, o * rad:o * rad + lp].reshape(bn, nh, dilation, nblk, rad, hd) for o in range(3)]
        return jnp.concatenate(parts, axis=4)

    qb = to_sub(q).reshape(bn, nh, dilation, nblk, rad, hd)
    kw = windows(to_sub(k))
    vw = windows(to_sub(v))
    rel = np.arange(3 * rad)[None, :] - rad - np.arange(rad)[:, None]
    bias = jnp.transpose(bias_tab[jnp.asarray(t5_bucket(rel * dilation))], (2, 0, 1)).astype(jnp.float32)
    kpos = np.arange(nblk)[:, None] * rad - rad + np.arange(3 * rad)[None, :]
    valid = (np.abs(rel) <= rad)[None] & (kpos[:, None, :] >= 0) & (kpos[:, None, :] < sub_len)
    s = jnp.einsum('bhrnqd,bhrnkd->bhrnqk', qb, kw).astype(jnp.float32) * (hd ** -0.5)
    s = jnp.where(valid, s + bias[None, :, None, None], NEG_INF)
    m = jnp.max(s, axis=-1, keepdims=True)
    p = jnp.exp(s - m)
    den = jnp.sum(p, axis=-1, keepdims=True)
    o = jnp.einsum('bhrnqk,bhrnkd->bhrnqd', p, vw.astype(jnp.float32)) / den
    lse = m + jnp.log(den)
    o = o.reshape(bn, nh, dilation, lp, hd)[:, :, :, :sub_len].transpose(0, 3, 2, 1, 4).reshape(bn, s_len, nh, hd)
    lse = lse.reshape(bn, nh, dilation, lp)[:, :, :, :sub_len].transpose(0, 3, 2, 1).reshape(bn, s_len, nh)
    return o, lse


def dilated_mixer(q, k, v, t5_table):
    bn, s_len = q.shape[0], q.shape[1]
    outs, lses = [], []
    for g, (window, dilation) in enumerate(DIL_GROUPS):
        sl = slice(g * A_HEADS_PER_GROUP, (g + 1) * A_HEADS_PER_GROUP)
        o, l = dilated_group(q[:, :, sl], k[:, :, sl], v[:, :, sl], t5_table[:, sl], window, dilation)
        outs.append(o)
        lses.append(l)
    o = jnp.stack(outs, axis=0)
    w = jax.nn.softmax(jnp.stack(lses, axis=0), axis=0)
    return jnp.sum(w[..., None] * o, axis=0).reshape(bn, s_len, A_OUT_WIDTH)


def neighbourhood_mixer(q, k, v, rpb):
    bn, s_len, nh, hd = q.shape
    rows = s_len // GRID_W
    kh = min(NB_ROWS, rows)
    n_cb = GRID_W // NB_Q_COLS

    def to_grid(t):
        return t.reshape(bn, rows, GRID_W, nh, hd).transpose(1, 0, 3, 2, 4)

    qg, kg, vg = to_grid(q), to_grid(k), to_grid(v)
    cols = np.arange(GRID_W)
    col_start = np.clip(cols - NB_COLS // 2, 0, GRID_W - NB_COLS).reshape(n_cb, NB_Q_COLS)
    blk_start = np.clip(np.arange(n_cb) * NB_Q_COLS - NB_COLS // 2, 0, GRID_W - NB_K_COLS)
    kc = blk_start[:, None] + np.arange(NB_K_COLS)[None, :]
    qc = cols.reshape(n_cb, NB_Q_COLS)
    col_valid = (kc[:, None, :] >= col_start[:, :, None]) & (kc[:, None, :] < col_start[:, :, None] + NB_COLS)
    mask = np.broadcast_to(col_valid[:, :, None, :], (n_cb, NB_Q_COLS, kh, NB_K_COLS)).reshape(n_cb, NB_Q_COLS, kh * NB_K_COLS)
    dc_idx = np.clip(kc[:, None, :] - qc[:, :, None] + NB_COLS - 1, 0, 2 * NB_COLS - 2)
    row_off = jnp.arange(kh)

    def row_fn(r):
        rs = jnp.clip(r - kh // 2, 0, rows - kh)
        kr = lax.dynamic_slice_in_dim(kg, rs, kh, axis=0)
        vr = lax.dynamic_slice_in_dim(vg, rs, kh, axis=0)

        def col_blocks(t):
            t = jnp.stack([t[:, :, :, int(s0):int(s0) + NB_K_COLS] for s0 in blk_start], axis=3)
            return t.transpose(1, 2, 3, 0, 4, 5).reshape(bn, nh, n_cb, kh * NB_K_COLS, hd)

        kb, vb = col_blocks(kr), col_blocks(vr)
        qr = lax.dynamic_index_in_dim(qg, r, axis=0, keepdims=False).reshape(bn, nh, n_cb, NB_Q_COLS, hd)
        dr = rs + row_off - r + NB_ROWS - 1
        row_tab = rpb[:, dr, :]
        bias = row_tab[:, :, dc_idx].transpose(0, 2, 3, 1, 4).reshape(nh, n_cb, NB_Q_COLS, kh * NB_K_COLS)
        s = jnp.einsum('bhcqd,bhckd->bhcqk', qr, kb).astype(jnp.float32) * (hd ** -0.5)
        s = jnp.where(mask, s + bias[None].astype(jnp.float32), NEG_INF)
        p = jax.nn.softmax(s, axis=-1)
        o = jnp.einsum('bhcqk,bhckd->bhcqd', p, vb.astype(jnp.float32))
        return o.reshape(bn, nh, GRID_W, hd)

    out = lax.map(row_fn, jnp.arange(rows))
    return out.transpose(1, 0, 3, 2, 4).reshape(bn, s_len, nh * hd)


def mixer_block(u, w_in, w_proj_a, w_proj_b, w_out, t5_table, rpb):
    bn, s_len, _ = u.shape
    z = u @ w_in
    splits = [int(c) for c in np.cumsum([A_WIDTH] * 3 + [B_WIDTH] * 3 + [D_MODEL])]
    qa, ka, va, qb, kb, vb, ga, gb = jnp.split(z, splits, axis=-1)
    heads_a = lambda t: t.reshape(bn, s_len, A_HEADS, HEAD_DIM)
    heads_b = lambda t: t.reshape(bn, s_len, B_HEADS, HEAD_DIM)
    oa = dilated_mixer(heads_a(qa), heads_a(ka), heads_a(va), t5_table).astype(u.dtype)
    ob = neighbourhood_mixer(heads_b(qb), heads_b(kb), heads_b(vb), rpb).astype(u.dtype)
    merged = jax.nn.sigmoid(ga) * (oa @ w_proj_a) + jax.nn.sigmoid(gb) * (ob @ w_proj_b)
    return merged @ w_out


def peer(u, w_query, sub_keys, expert_u, expert_v):
    bn, s_len, d = u.shape
    xt = u.reshape(-1, d)
    n_tok = xt.shape[0]
    qr = (xt @ w_query).reshape(n_tok, PEER_HEADS, 2, PEER_HALF)
    s = jnp.einsum('thpc,pnc->thpn', qr, sub_keys).astype(jnp.float32)
    s_top, i_top = lax.top_k(s, PEER_TOPK)
    cand = (s_top[:, :, 0, :, None] + s_top[:, :, 1, None, :]).reshape(n_tok, PEER_HEADS, PEER_TOPK * PEER_TOPK)
    cand_idx = (i_top[:, :, 0, :, None] * PEER_NKEYS + i_top[:, :, 1, None, :]).reshape(n_tok, PEER_HEADS, PEER_TOPK * PEER_TOPK)
    best, pos = lax.top_k(cand, PEER_TOPK)
    idx = jnp.take_along_axis(cand_idx, pos, axis=-1).reshape(n_tok, PEER_HEADS * PEER_TOPK)
    gate = jax.nn.softmax(best, axis=-1).reshape(n_tok, PEER_HEADS * PEER_TOPK)
    n_chunk = n_tok // PEER_CHUNK

    def chunk_fn(args):
        xc, ic, gc = args
        act = jax.nn.gelu(jnp.einsum('cd,ced->ce', xc, expert_u[ic]).astype(jnp.float32))
        coef = (gc * act).astype(xc.dtype)
        return jnp.einsum('ce,ced->cd', coef, expert_v[ic])

    out = lax.map(chunk_fn, (xt.reshape(n_chunk, PEER_CHUNK, d),
                             idx.reshape(n_chunk, PEER_CHUNK, -1),
                             gate.reshape(n_chunk, PEER_CHUNK, -1)))
    return out.reshape(bn, s_len, d)


def trunk(x, norm_mix, w_in, w_proj_a, w_proj_b, w_out, rpb, norm_ffn, w_query, sub_keys,
          expert_u, expert_v, t5_table, norm_final):
    h = x
    for l in range(DEPTH):
        h = h + mixer_block(rmsnorm(h, norm_mix[l]), w_in[l], w_proj_a[l], w_proj_b[l], w_out[l], t5_table, rpb[l])
        h = h + peer(rmsnorm(h, norm_ffn[l]), w_query[l], sub_keys[l], expert_u[l], expert_v[l])
    return rmsnorm(h, norm_final)


def setup_inputs(seed: int = 0) -> dict:
    key = jax.random.key(seed)
    ks = jax.random.split(key, 16)
    nrm = lambda k, shape, scale: jax.random.normal(k, shape, jnp.float32) * scale
    return {
        "x_prompt": nrm(ks[0], (BATCH, SEQ, D_MODEL), 1.0),
        "x_sample": nrm(ks[1], (DEC_BATCH, DEC_SEQ, D_MODEL), 1.0),
        "norm_mix": 1.0 + nrm(ks[2], (DEPTH, D_MODEL), 0.05),
        "w_in": nrm(ks[3], (DEPTH, D_MODEL, IN_WIDTH), D_MODEL ** -0.5),
        "w_proj_a": nrm(ks[4], (DEPTH, A_OUT_WIDTH, D_MODEL), A_OUT_WIDTH ** -0.5),
        "w_proj_b": nrm(ks[5], (DEPTH, B_WIDTH, D_MODEL), B_WIDTH ** -0.5),
        "w_out": nrm(ks[6], (DEPTH, D_MODEL, D_MODEL), D_MODEL ** -0.5),
        "rpb": nrm(ks[7], (DEPTH, B_HEADS, 2 * NB_ROWS - 1, 2 * NB_COLS - 1), 0.2),
        "norm_ffn": 1.0 + nrm(ks[8], (DEPTH, D_MODEL), 0.05),
        "w_query": nrm(ks[9], (DEPTH, D_MODEL, PEER_HEADS * PEER_QDIM), D_MODEL ** -0.5),
        "sub_keys": nrm(ks[10], (DEPTH, 2, PEER_NKEYS, PEER_HALF), PEER_HALF ** -0.5),
        "expert_u": nrm(ks[11], (DEPTH, PEER_EXPERTS, D_MODEL), D_MODEL ** -0.5),
        "expert_v": nrm(ks[12], (DEPTH, PEER_EXPERTS, D_MODEL), PEER_HEADS ** -0.5),
        "t5_table": nrm(ks[13], (T5_BUCKETS, A_HEADS), 0.2),
        "norm_final": 1.0 + nrm(ks[14], (D_MODEL,), 0.05),
    }


def reference(x_prompt, x_sample, norm_mix, w_in, w_proj_a, w_proj_b, w_out, rpb, norm_ffn,
              w_query, sub_keys, expert_u, expert_v, t5_table, norm_final):
    y_prompt = trunk(x_prompt, norm_mix, w_in, w_proj_a, w_proj_b, w_out, rpb, norm_ffn, w_query,
                     sub_keys, expert_u, expert_v, t5_table, norm_final)
    y_sample = trunk(x_sample, norm_mix, w_in, w_proj_a, w_proj_b, w_out, rpb, norm_ffn, w_query,
                     sub_keys, expert_u, expert_v, t5_table, norm_final)
    return (y_prompt, y_sample)
```

```python
import functools
import math

import numpy as np
import jax
import jax.numpy as jnp
from jax import lax
from jax.experimental import pallas as pl
from jax.experimental.pallas import tpu as pltpu

D_MODEL = 1024
HEAD_DIM = 64
DIL_GROUPS = ((128, 1), (512, 4), (2048, 16))
A_HEADS_PER_GROUP = 4
A_HEADS = A_HEADS_PER_GROUP * len(DIL_GROUPS)
A_WIDTH = A_HEADS * HEAD_DIM
A_OUT_WIDTH = A_HEADS_PER_GROUP * HEAD_DIM
T5_BUCKETS = 32
T5_MAX_DIST = 1024
GRID_W = 64
B_HEADS = 8
B_WIDTH = B_HEADS * HEAD_DIM
NB_ROWS = 8
NB_COLS = 16
PEER_HEADS = 8
PEER_NKEYS = 128
PEER_EXPERTS = PEER_NKEYS * PEER_NKEYS
PEER_QDIM = 256
PEER_HALF = PEER_QDIM // 2
PEER_TOPK = 16
PEER_SEL = PEER_HEADS * PEER_TOPK
NORM_EPS = 1e-6
NEG_INF = -1e30

RAD = 64
SUBLANES = 8
LANES = 128
HALF_EXPERTS = PEER_EXPERTS // 2
VMEM_LIMIT = 48 * 1024 * 1024

F32 = jnp.float32
BF16 = jnp.bfloat16


def _resident(shape):
    nd = len(shape)
    return pl.BlockSpec(shape, lambda *_: (0,) * nd, pipeline_mode=pl.Buffered(1))


def _params(sem):
    return pltpu.CompilerParams(dimension_semantics=sem, vmem_limit_bytes=VMEM_LIMIT)


IN_SPLITS = (A_WIDTH, A_WIDTH, A_WIDTH, B_WIDTH, B_WIDTH, B_WIDTH, D_MODEL, D_MODEL)
IN_SCALES = (HEAD_DIM ** -0.5, 1.0, 1.0, HEAD_DIM ** -0.5, 1.0, 1.0, 1.0, 1.0)


def _rms(x, g):
    return x * lax.rsqrt(jnp.mean(x * x, axis=-1, keepdims=True) + NORM_EPS) * g


def _inproj_kernel(x_ref, g_ref, w_ref, *out_refs):
    u = _rms(x_ref[...], g_ref[...]).astype(BF16)
    off = 0
    for o_ref, width, scale in zip(out_refs, IN_SPLITS, IN_SCALES):
        z = jnp.dot(u, w_ref[:, off:off + width], preferred_element_type=F32)
        if scale != 1.0:
            z = z * scale
        o_ref[...] = z.astype(o_ref.dtype)
        off += width


def _inproj(x2, g, w_bf16, tm):
    n_tok = x2.shape[0]
    return pl.pallas_call(
        _inproj_kernel,
        grid=(n_tok // tm,),
        in_specs=[pl.BlockSpec((tm, D_MODEL), lambda i: (i, 0)),
                  _resident((1, D_MODEL)),
                  _resident(w_bf16.shape)],
        out_specs=[pl.BlockSpec((tm, w), lambda i: (i, 0)) for w in IN_SPLITS],
        out_shape=[jax.ShapeDtypeStruct((n_tok, w), BF16) for w in IN_SPLITS],
        compiler_params=_params(("parallel",)),
        name="inproj",
    )(x2, g, w_bf16)


def _t5_bucket(rel):
    n = -rel
    nb = T5_BUCKETS // 2
    ret = (n < 0).astype(np.int32) * nb
    n = np.abs(n)
    max_exact = nb // 2
    large = max_exact + (np.log(np.maximum(n, 1) / max_exact) / math.log(T5_MAX_DIST / max_exact)
                         * (nb - max_exact)).astype(np.int32)
    large = np.minimum(large, nb - 1)
    return (ret + np.where(n < max_exact, n, large)).astype(np.int32)


def _dilated_bias(t5_table, group, dilation, tq):
    rel = np.arange(tq + 2 * RAD)[None, :] - RAD - np.arange(tq)[:, None]
    tab = t5_table[:, group * A_HEADS_PER_GROUP:(group + 1) * A_HEADS_PER_GROUP].astype(F32)
    bias = jnp.transpose(tab[jnp.asarray(_t5_bucket(rel * dilation))], (2, 0, 1))
    return jnp.where(jnp.asarray(np.abs(rel) <= RAD)[None], bias, NEG_INF)


def _dilated_kernel(q_ref, kp_ref, kc_ref, kn_ref, vp_ref, vc_ref, vn_ref, bias_ref, o_ref, l_ref, *, tq, n_tiles):
    i = pl.program_id(2)
    q = q_ref[0]
    k_all = jnp.concatenate([kp_ref[0], kc_ref[0], kn_ref[0]], axis=0)
    v_all = jnp.concatenate([vp_ref[0], vc_ref[0], vn_ref[0]], axis=0)
    nk = tq + 2 * RAD
    col = lax.broadcasted_iota(jnp.int32, (tq, nk), 1)
    first_valid = jnp.where(i == 0, RAD, 0)
    end_valid = jnp.where(i == n_tiles - 1, tq + RAD, nk)
    valid = (col >= first_valid) & (col < end_valid)
    lane = lax.broadcasted_iota(jnp.int32, (tq, A_OUT_WIDTH), 1)
    o_acc = jnp.zeros((tq, A_OUT_WIDTH), F32)
    l_acc = jnp.zeros((tq, A_OUT_WIDTH), F32)
    for h in range(A_HEADS_PER_GROUP):
        head = (lane >= h * HEAD_DIM) & (lane < (h + 1) * HEAD_DIM)
        qh = jnp.where(head, q, jnp.zeros_like(q))
        s = lax.dot_general(qh, k_all, (((1,), (1,)), ((), ())), preferred_element_type=F32)
        s = jnp.where(valid, s + bias_ref[h], NEG_INF)
        m = jnp.max(s, axis=-1, keepdims=True)
        p = jnp.exp(s - m)
        den = jnp.sum(p, axis=-1, keepdims=True)
        o = jnp.dot(p.astype(BF16), v_all, preferred_element_type=F32) / den
        lse = m + jnp.log(den)
        o_acc = jnp.where(head, o, o_acc)
        l_acc = jnp.where(head, lse, l_acc)
    o_ref[0] = o_acc
    l_ref[0] = l_acc


def _dilated_group(qa, ka, va, bias, group, dilation, tq):
    bn, s_len, _ = qa.shape
    sub_len = s_len // dilation
    n_tiles = sub_len // tq
    blk = tq // RAD
    n_rad = sub_len // RAD
    ngrp = len(DIL_GROUPS)
    view = lambda t: t.reshape(bn, sub_len, dilation * A_WIDTH)
    cur = pl.BlockSpec((1, tq, A_OUT_WIDTH), lambda b, r, i: (b, i, r * ngrp + group))
    prev = pl.BlockSpec((1, RAD, A_OUT_WIDTH), lambda b, r, i: (b, jnp.maximum(i * blk - 1, 0), r * ngrp + group))
    nxt = pl.BlockSpec((1, RAD, A_OUT_WIDTH), lambda b, r, i: (b, jnp.minimum((i + 1) * blk, n_rad - 1), r * ngrp + group))
    out = pl.BlockSpec((1, tq, A_OUT_WIDTH), lambda b, r, i: (b, i, r))
    o, l = pl.pallas_call(
        functools.partial(_dilated_kernel, tq=tq, n_tiles=n_tiles),
        grid=(bn, dilation, n_tiles),
        in_specs=[cur, prev, cur, nxt, prev, cur, nxt, _resident(bias.shape)],
        out_specs=[out, out],
        out_shape=[jax.ShapeDtypeStruct((bn, sub_len, dilation * A_OUT_WIDTH), F32)] * 2,
        compiler_params=_params(("parallel", "parallel", "parallel")),
        name=f"dilated{group}",
    )(view(qa), view(ka), view(ka), view(ka), view(va), view(va), view(va), bias)
    return o.reshape(bn, s_len, A_OUT_WIDTH), l.reshape(bn, s_len, A_OUT_WIDTH)


NA_TILE_ROWS = 8
NA_TILE = NA_TILE_ROWS * GRID_W
NA_KEYS = NB_ROWS * GRID_W


def _na_bias(rpb):
    cols = np.arange(GRID_W)
    col_start = np.clip(cols - NB_COLS // 2, 0, GRID_W - NB_COLS)
    col_valid = (cols[None, :] >= col_start[:, None]) & (cols[None, :] < col_start[:, None] + NB_COLS)
    dc = np.clip(cols[None, :] - cols[:, None] + NB_COLS - 1, 0, 2 * NB_COLS - 2)
    rpb = rpb.astype(F32)
    variants = []
    for v in range(NB_ROWS):
        dr = np.clip(np.arange(NB_ROWS) + v, 0, 2 * NB_ROWS - 2)
        tab = rpb[:, dr][:, :, dc]
        tab = jnp.where(jnp.asarray(col_valid)[None, None], tab, NEG_INF)
        variants.append(jnp.transpose(tab, (0, 2, 1, 3)).reshape(B_HEADS, GRID_W, NA_KEYS))
    return jnp.stack(variants, axis=0)


def _na_kernel(q_ref, kp_ref, kc_ref, kn_ref, vp_ref, vc_ref, vn_ref, bias_ref, o_ref, kwin, vwin, *, rows):
    i = pl.program_id(1)
    kwin[0:NA_TILE] = kp_ref[0]
    kwin[NA_TILE:2 * NA_TILE] = kc_ref[0]
    kwin[2 * NA_TILE:3 * NA_TILE] = kn_ref[0]
    vwin[0:NA_TILE] = vp_ref[0]
    vwin[NA_TILE:2 * NA_TILE] = vc_ref[0]
    vwin[2 * NA_TILE:3 * NA_TILE] = vn_ref[0]
    lane = lax.broadcasted_iota(jnp.int32, (GRID_W, 2 * HEAD_DIM), 1)
    low = lane < HEAD_DIM
    for a in range(NA_TILE_ROWS):
        r = i * NA_TILE_ROWS + a
        rs = jnp.clip(r - NB_ROWS // 2, 0, rows - NB_ROWS)
        start = pl.multiple_of((rs - (i - 1) * NA_TILE_ROWS) * GRID_W, GRID_W)
        variant = rs - r + NB_ROWS - 1
        for hp in range(B_HEADS // 2):
            cs = slice(hp * 2 * HEAD_DIM, (hp + 1) * 2 * HEAD_DIM)
            q2 = q_ref[0, a * GRID_W:(a + 1) * GRID_W, cs]
            k2 = kwin[pl.ds(start, NA_KEYS), cs]
            v2 = vwin[pl.ds(start, NA_KEYS), cs]
            o2 = None
            for hh in range(2):
                mine = low if hh == 0 else jnp.logical_not(low)
                qh = jnp.where(mine, q2, jnp.zeros_like(q2))
                s = lax.dot_general(qh, k2, (((1,), (1,)), ((), ())), preferred_element_type=F32)
                s = s + bias_ref[variant, hp * 2 + hh]
                m = jnp.max(s, axis=-1, keepdims=True)
                p = jnp.exp(s - m)
                den = jnp.sum(p, axis=-1, keepdims=True)
                o = jnp.dot(p.astype(BF16), v2, preferred_element_type=F32) / den
                o2 = o if hh == 0 else jnp.where(low, o2, o)
            o_ref[0, a * GRID_W:(a + 1) * GRID_W, cs] = o2.astype(o_ref.dtype)


def _neighbourhood(qb, kb, vb, bias):
    bn, s_len, _ = qb.shape
    rows = s_len // GRID_W
    n_tiles = rows // NA_TILE_ROWS
    cur = pl.BlockSpec((1, NA_TILE, B_WIDTH), lambda b, i: (b, i, 0))
    prev = pl.BlockSpec((1, NA_TILE, B_WIDTH), lambda b, i: (b, jnp.maximum(i - 1, 0), 0))
    nxt = pl.BlockSpec((1, NA_TILE, B_WIDTH), lambda b, i: (b, jnp.minimum(i + 1, n_tiles - 1), 0))
    return pl.pallas_call(
        functools.partial(_na_kernel, rows=rows),
        grid=(bn, n_tiles),
        in_specs=[cur, prev, cur, nxt, prev, cur, nxt, _resident(bias.shape)],
        out_specs=cur,
        out_shape=jax.ShapeDtypeStruct((bn, s_len, B_WIDTH), BF16),
        scratch_shapes=[pltpu.VMEM((3 * NA_TILE, B_WIDTH), BF16)] * 2,
        compiler_params=_params(("parallel", "parallel")),
        name="neighbourhood",
    )(qb, kb, kb, kb, vb, vb, vb, bias)


def _merge_kernel(x_ref, o1_ref, l1_ref, o2_ref, l2_ref, o3_ref, l3_ref, ob_ref, ga_ref, gb_ref,
                  wpa_ref, wpb_ref, wout_ref, gffn_ref, wq_ref, h_ref, u_ref, qp_ref):
    l1, l2, l3 = l1_ref[...], l2_ref[...], l3_ref[...]
    m = jnp.maximum(jnp.maximum(l1, l2), l3)
    w1, w2, w3 = jnp.exp(l1 - m), jnp.exp(l2 - m), jnp.exp(l3 - m)
    oa = (w1 * o1_ref[...] + w2 * o2_ref[...] + w3 * o3_ref[...]) / (w1 + w2 + w3)
    pa = jnp.dot(oa.astype(BF16), wpa_ref[...], preferred_element_type=F32)
    pb = jnp.dot(ob_ref[...], wpb_ref[...], preferred_element_type=F32)
    merged = jax.nn.sigmoid(ga_ref[...].astype(F32)) * pa + jax.nn.sigmoid(gb_ref[...].astype(F32)) * pb
    h = x_ref[...] + jnp.dot(merged.astype(BF16), wout_ref[...], preferred_element_type=F32)
    u = _rms(h, gffn_ref[...])
    h_ref[...] = h
    u_ref[...] = u
    qp_ref[...] = jnp.dot(u.astype(BF16), wq_ref[...], preferred_element_type=F32).astype(qp_ref.dtype)


def _merge(x2, dil_outs, ob, ga, gb, wpa, wpb, wout, gffn, wq, tm):
    n_tok = x2.shape[0]
    tok = lambda w: pl.BlockSpec((tm, w), lambda i: (i, 0))
    ins = [x2]
    specs = [tok(D_MODEL)]
    for o, l in dil_outs:
        ins += [o, l]
        specs += [tok(A_OUT_WIDTH), tok(A_OUT_WIDTH)]
    ins += [ob, ga, gb, wpa, wpb, wout, gffn, wq]
    specs += [tok(B_WIDTH), tok(D_MODEL), tok(D_MODEL), _resident(wpa.shape), _resident(wpb.shape),
              _resident(wout.shape), _resident(gffn.shape), _resident(wq.shape)]
    qw = PEER_HEADS * PEER_QDIM
    return pl.pallas_call(
        _merge_kernel,
        grid=(n_tok // tm,),
        in_specs=specs,
        out_specs=[tok(D_MODEL), tok(D_MODEL), tok(qw)],
        out_shape=[jax.ShapeDtypeStruct((n_tok, D_MODEL), F32), jax.ShapeDtypeStruct((n_tok, D_MODEL), F32),
                   jax.ShapeDtypeStruct((n_tok, qw), BF16)],
        compiler_params=_params(("parallel",)),
        name="merge",
    )(*ins)


def _take_top(s, iota, count, payload=None):
    n = s.shape[0]
    vals, picks = [], []
    for _ in range(count):
        m = jnp.max(s, axis=0, keepdims=True)
        pos = jnp.min(jnp.where(s == m, iota, n), axis=0, keepdims=True)
        hit = iota == pos
        vals.append(m)
        picks.append(pos if payload is None else jnp.max(jnp.where(hit, payload, -1), axis=0, keepdims=True))
        s = jnp.where(hit, -jnp.inf, s)
    return vals, picks


def _topk_kernel(qp_ref, sk_ref, row_ref, sh_ref, gate_ref):
    tm = qp_ref.shape[0]
    iota = lax.broadcasted_iota(jnp.int32, (PEER_NKEYS, tm), 0)
    tops = []
    for p in range(2):
        q = qp_ref[:, p * PEER_HALF:(p + 1) * PEER_HALF]
        s = lax.dot_general(sk_ref[p], q, (((1,), (1,)), ((), ())), preferred_element_type=F32)
        tops.append(_take_top(s, iota, PEER_TOPK))
    (va, ia), (vb, ib) = tops
    vb_all = jnp.concatenate(vb, axis=0)
    ib_all = jnp.concatenate(ib, axis=0)
    cand = jnp.concatenate([va[a] + vb_all for a in range(PEER_TOPK)], axis=0)
    cidx = jnp.concatenate([ia[a] * PEER_NKEYS + ib_all for a in range(PEER_TOPK)], axis=0)
    iota2 = lax.broadcasted_iota(jnp.int32, (PEER_TOPK * PEER_TOPK, tm), 0)
    best, experts = _take_top(cand, iota2, PEER_TOPK, payload=cidx)
    best = jnp.concatenate(best, axis=0)
    e = jnp.exp(best - best[0:1])
    gate_ref[0] = e / jnp.sum(e, axis=0, keepdims=True)
    experts = jnp.concatenate(experts, axis=0)
    row_ref[0] = experts & (HALF_EXPERTS - 1)
    sh_ref[0] = jnp.where(experts < HALF_EXPERTS, 16, 0)


def _topk(qp, sk_bf16, tm):
    n_tok = qp.shape[0]
    n_tiles = n_tok // tm
    out = pl.BlockSpec((1, PEER_TOPK, tm), lambda i, h: (i, h, 0))
    shape = (n_tiles, PEER_SEL, tm)
    return pl.pallas_call(
        _topk_kernel,
        grid=(n_tiles, PEER_HEADS),
        in_specs=[pl.BlockSpec((tm, PEER_QDIM), lambda i, h: (i, h)), _resident(sk_bf16.shape)],
        out_specs=[out, out, out],
        out_shape=[jax.ShapeDtypeStruct(shape, jnp.int32), jax.ShapeDtypeStruct(shape, jnp.int32),
                   jax.ShapeDtypeStruct(shape, F32)],
        compiler_params=_params(("parallel", "parallel")),
        name="peer_topk",
    )(qp, sk_bf16)


HIGH_HALF = -65536


def _pack_experts(tbl):
    b = lax.bitcast_convert_type(tbl.astype(BF16), jnp.uint16).astype(jnp.uint32)
    w = (b[HALF_EXPERTS:] << 16) | b[:HALF_EXPERTS]
    return lax.bitcast_convert_type(w, jnp.int32).reshape(HALF_EXPERTS, SUBLANES, LANES)


def _expert_row(tbl_ref, row, shift):
    w = tbl_ref[row]
    return lax.bitcast_convert_type((w << shift) & HIGH_HALF, F32)


def _sublane_sums(parts, masks):
    shift = SUBLANES // 2
    for mask in masks:
        half = len(parts) // 2
        nxt = []
        for j in range(half):
            a, b = parts[j], parts[j + half]
            nxt.append(jnp.where(mask, a, pltpu.roll(b, shift, 0)) + jnp.where(mask, pltpu.roll(a, SUBLANES - shift, 0), b))
        parts = nxt
        shift //= 2
    return parts[0]


def _peer_u_kernel(row_sm, sh_sm, x_ref, gate_ref, tbl_ref, coef_ref):
    tm = x_ref.shape[0]
    sub = lax.broadcasted_iota(jnp.int32, (SUBLANES, LANES), 0)
    masks = (sub < 4, (sub & 3) < 2, (sub & 1) == 0)
    lane = lax.broadcasted_iota(jnp.int32, (PEER_SEL, tm), 1)

    def token(t, act):
        x = x_ref[t]
        groups = []
        for g in range(PEER_SEL // SUBLANES):
            prods = []
            for j in range(SUBLANES):
                k = g * SUBLANES + j
                prods.append(_expert_row(tbl_ref, row_sm[0, k, t], sh_sm[0, k, t]) * x)
            groups.append(_sublane_sums(prods, masks))
        dots = jnp.sum(jnp.concatenate(groups, axis=0), axis=-1, keepdims=True)
        return jnp.where(lane == t, dots, act)

    act = lax.fori_loop(0, tm, token, jnp.zeros((PEER_SEL, tm), F32))
    coef_ref[0] = gate_ref[0] * jax.nn.gelu(act)


def _peer_v_kernel(row_sm, sh_sm, coef_sm, h_ref, g_ref, tbl_ref, y_ref):
    tm = h_ref.shape[0]
    n_acc = 4

    def token(t, carry):
        accs = [jnp.zeros((SUBLANES, LANES), F32) for _ in range(n_acc)]
        for k in range(PEER_SEL):
            v = _expert_row(tbl_ref, row_sm[0, k, t], sh_sm[0, k, t])
            accs[k % n_acc] = accs[k % n_acc] + v * coef_sm[0, k, t]
        h = h_ref[t] + ((accs[0] + accs[1]) + (accs[2] + accs[3]))
        ms = jnp.sum(jnp.sum(h * h, axis=1, keepdims=True), axis=0, keepdims=True) * (1.0 / D_MODEL)
        y_ref[t] = h * lax.rsqrt(ms + NORM_EPS) * g_ref[...]
        return carry

    lax.fori_loop(0, tm, token, 0)


def _smem_tile(tm):
    return pl.BlockSpec((1, PEER_SEL, tm), lambda i: (i, 0, 0), memory_space=pltpu.SMEM)


def _peer_u(rows, shifts, gates, u3, tbl, tm):
    n_tiles = rows.shape[0]
    vm = pl.BlockSpec((1, PEER_SEL, tm), lambda i: (i, 0, 0))
    return pl.pallas_call(
        _peer_u_kernel,
        grid=(n_tiles,),
        in_specs=[_smem_tile(tm), _smem_tile(tm), pl.BlockSpec((tm, SUBLANES, LANES), lambda i: (i, 0, 0)), vm,
                  _resident(tbl.shape)],
        out_specs=vm,
        out_shape=jax.ShapeDtypeStruct((n_tiles, PEER_SEL, tm), F32),
        compiler_params=_params(("arbitrary",)),
        name="peer_u",
    )(rows, shifts, u3, gates, tbl)


def _peer_v(rows, shifts, coefs, h3, g_final, tbl, tm):
    n_tiles = rows.shape[0]
    tok = pl.BlockSpec((tm, SUBLANES, LANES), lambda i: (i, 0, 0))
    return pl.pallas_call(
        _peer_v_kernel,
        grid=(n_tiles,),
        in_specs=[_smem_tile(tm), _smem_tile(tm), _smem_tile(tm), tok, _resident(g_final.shape), _resident(tbl.shape)],
        out_specs=tok,
        out_shape=jax.ShapeDtypeStruct(h3.shape, F32),
        compiler_params=_params(("arbitrary",)),
        name="peer_v",
    )(rows, shifts, coefs, h3, g_final, tbl)


PROJ_TILE = 512
MERGE_TILE = 256
PEER_TILE = 128


def _trunk(x, p):
    bn, s_len, d = x.shape
    n_tok = bn * s_len
    x2 = x.reshape(n_tok, d)
    qa, ka, va, qb, kb, vb, ga, gb = _inproj(x2, p["norm_mix"], p["w_in"], PROJ_TILE)
    seq = lambda t: t.reshape(bn, s_len, t.shape[-1])
    dil_outs = []
    for group, (window, dilation) in enumerate(DIL_GROUPS):
        assert window // (2 * dilation) == RAD
        sub_len = s_len // dilation
        assert s_len % dilation == 0 and sub_len % RAD == 0
        tq = min(256, sub_len)
        assert sub_len % tq == 0
        bias = _dilated_bias(p["t5_table"], group, dilation, tq)
        o, l = _dilated_group(seq(qa), seq(ka), seq(va), bias, group, dilation, tq)
        dil_outs.append((o.reshape(n_tok, A_OUT_WIDTH), l.reshape(n_tok, A_OUT_WIDTH)))
    assert s_len % NA_TILE == 0 and s_len // GRID_W >= NB_ROWS
    ob = _neighbourhood(seq(qb), seq(kb), seq(vb), p["na_bias"]).reshape(n_tok, B_WIDTH)
    h, u, qp = _merge(x2, dil_outs, ob, ga, gb, p["w_proj_a"], p["w_proj_b"], p["w_out"], p["norm_ffn"],
                      p["w_query"], MERGE_TILE)
    rows, shifts, gates = _topk(qp, p["sub_keys"], PEER_TILE)
    tile3 = lambda t: t.reshape(n_tok, SUBLANES, LANES)
    coefs = _peer_u(rows, shifts, gates, tile3(u), p["expert_u"], PEER_TILE)
    y = _peer_v(rows, shifts, coefs, tile3(h), p["norm_final"], p["expert_v"], PEER_TILE)
    return y.reshape(bn, s_len, d)


def kernel(x_prompt, x_sample, norm_mix, w_in, w_proj_a, w_proj_b, w_out, rpb, norm_ffn, w_query, sub_keys,
           expert_u, expert_v, t5_table, norm_final):
    assert norm_mix.shape[0] == 1, "single-layer model"
    p = {
        "norm_mix": norm_mix[0].reshape(1, D_MODEL).astype(F32),
        "w_in": w_in[0].astype(BF16),
        "w_proj_a": w_proj_a[0].astype(BF16),
        "w_proj_b": w_proj_b[0].astype(BF16),
        "w_out": w_out[0].astype(BF16),
        "na_bias": _na_bias(rpb[0]),
        "norm_ffn": norm_ffn[0].reshape(1, D_MODEL).astype(F32),
        "w_query": w_query[0].astype(BF16),
        "sub_keys": sub_keys[0].astype(BF16),
        "expert_u": _pack_experts(expert_u[0]),
        "expert_v": _pack_experts(expert_v[0]),
        "t5_table": t5_table,
        "norm_final": norm_final.reshape(SUBLANES, LANES).astype(F32),
    }
    return (_trunk(x_prompt, p), _trunk(x_sample, p))
```

```python
import functools
import math

import numpy as np
import jax
import jax.numpy as jnp
from jax import lax
from jax.experimental import pallas as pl
from jax.experimental.pallas import tpu as pltpu

D_MODEL = 1024
HEAD_DIM = 64
DIL_GROUPS = ((128, 1), (512, 4), (2048, 16))
A_HEADS_PER_GROUP = 4
A_HEADS = A_HEADS_PER_GROUP * len(DIL_GROUPS)
A_WIDTH = A_HEADS * HEAD_DIM
A_OUT_WIDTH = A_HEADS_PER_GROUP * HEAD_DIM
T5_BUCKETS = 32
T5_MAX_DIST = 1024
GRID_W = 64
B_HEADS = 8
B_WIDTH = B_HEADS * HEAD_DIM
NB_ROWS = 8
NB_COLS = 16
PEER_HEADS = 8
PEER_NKEYS = 128
PEER_EXPERTS = PEER_NKEYS * PEER_NKEYS
PEER_QDIM = 256
PEER_HALF = PEER_QDIM // 2
PEER_TOPK = 16
PEER_SEL = PEER_HEADS * PEER_TOPK
NORM_EPS = 1e-6
NEG_INF = -1e30

RAD = 64
SUBLANES = 8
LANES = 128
HALF_EXPERTS = PEER_EXPERTS // 2
VMEM_LIMIT = 48 * 1024 * 1024

F32 = jnp.float32
BF16 = jnp.bfloat16


def _resident(shape):
    nd = len(shape)
    return pl.BlockSpec(shape, lambda *_: (0,) * nd, pipeline_mode=pl.Buffered(1))


def _params(sem):
    return pltpu.CompilerParams(dimension_semantics=sem, vmem_limit_bytes=VMEM_LIMIT)


IN_SPLITS = (A_WIDTH, A_WIDTH, A_WIDTH, B_WIDTH, B_WIDTH, B_WIDTH, D_MODEL, D_MODEL)
IN_SCALES = (HEAD_DIM ** -0.5, 1.0, 1.0, HEAD_DIM ** -0.5, 1.0, 1.0, 1.0, 1.0)


def _rms(x, g):
    return x * lax.rsqrt(jnp.mean(x * x, axis=-1, keepdims=True) + NORM_EPS) * g


def _inproj_kernel(x_ref, g_ref, w_ref, *out_refs):
    u = _rms(x_ref[...], g_ref[...]).astype(BF16)
    off = 0
    for o_ref, width, scale in zip(out_refs, IN_SPLITS, IN_SCALES):
        z = jnp.dot(u, w_ref[:, off:off + width], preferred_element_type=F32)
        if scale != 1.0:
            z = z * scale
        o_ref[...] = z.astype(o_ref.dtype)
        off += width


def _inproj(x2, g, w_bf16, tm):
    n_tok = x2.shape[0]
    return pl.pallas_call(
        _inproj_kernel,
        grid=(n_tok // tm,),
        in_specs=[pl.BlockSpec((tm, D_MODEL), lambda i: (i, 0)),
                  _resident((1, D_MODEL)),
                  _resident(w_bf16.shape)],
        out_specs=[pl.BlockSpec((tm, w), lambda i: (i, 0)) for w in IN_SPLITS],
        out_shape=[jax.ShapeDtypeStruct((n_tok, w), BF16) for w in IN_SPLITS],
        compiler_params=_params(("parallel",)),
        name="inproj",
    )(x2, g, w_bf16)


def _t5_bucket(rel):
    n = -rel
    nb = T5_BUCKETS // 2
    ret = (n < 0).astype(np.int32) * nb
    n = np.abs(n)
    max_exact = nb // 2
    large = max_exact + (np.log(np.maximum(n, 1) / max_exact) / math.log(T5_MAX_DIST / max_exact)
                         * (nb - max_exact)).astype(np.int32)
    large = np.minimum(large, nb - 1)
    return (ret + np.where(n < max_exact, n, large)).astype(np.int32)


def _dilated_bias(t5_table, group, dilation, tq):
    rel = np.arange(tq + 2 * RAD)[None, :] - RAD - np.arange(tq)[:, None]
    tab = t5_table[:, group * A_HEADS_PER_GROUP:(group + 1) * A_HEADS_PER_GROUP].astype(F32)
    bias = jnp.transpose(tab[jnp.asarray(_t5_bucket(rel * dilation))], (2, 0, 1))
    return jnp.where(jnp.asarray(np.abs(rel) <= RAD)[None], bias, NEG_INF)


def _dilated_kernel(q_ref, kp_ref, kc_ref, kn_ref, vp_ref, vc_ref, vn_ref, bias_ref, o_ref, l_ref, *, tq, n_tiles):
    i = pl.program_id(2)
    q = q_ref[0]
    k_all = jnp.concatenate([kp_ref[0], kc_ref[0], kn_ref[0]], axis=0)
    v_all = jnp.concatenate([vp_ref[0], vc_ref[0], vn_ref[0]], axis=0)
    nk = tq + 2 * RAD
    col = lax.broadcasted_iota(jnp.int32, (tq, nk), 1)
    first_valid = jnp.where(i == 0, RAD, 0)
    end_valid = jnp.where(i == n_tiles - 1, tq + RAD, nk)
    valid = (col >= first_valid) & (col < end_valid)
    lane = lax.broadcasted_iota(jnp.int32, (tq, A_OUT_WIDTH), 1)
    o_acc = jnp.zeros((tq, A_OUT_WIDTH), F32)
    l_acc = jnp.zeros((tq, A_OUT_WIDTH), F32)
    for h in range(A_HEADS_PER_GROUP):
        head = (lane >= h * HEAD_DIM) & (lane < (h + 1) * HEAD_DIM)
        qh = jnp.where(head, q, jnp.zeros_like(q))
        s = lax.dot_general(qh, k_all, (((1,), (1,)), ((), ())), preferred_element_type=F32)
        s = jnp.where(valid, s + bias_ref[h], NEG_INF)
        m = jnp.max(s, axis=-1, keepdims=True)
        p = jnp.exp(s - m)
        den = jnp.sum(p, axis=-1, keepdims=True)
        o = jnp.dot(p.astype(BF16), v_all, preferred_element_type=F32) / den
        lse = m + jnp.log(den)
        o_acc = jnp.where(head, o, o_acc)
        l_acc = jnp.where(head, lse, l_acc)
    o_ref[0] = o_acc
    l_ref[0] = l_acc


def _dilated_group(qa, ka, va, bias, group, dilation, tq):
    bn, s_len, _ = qa.shape
    sub_len = s_len // dilation
    n_tiles = sub_len // tq
    blk = tq // RAD
    n_rad = sub_len // RAD
    ngrp = len(DIL_GROUPS)
    view = lambda t: t.reshape(bn, sub_len, dilation * A_WIDTH)
    cur = pl.BlockSpec((1, tq, A_OUT_WIDTH), lambda b, r, i: (b, i, r * ngrp + group))
    prev = pl.BlockSpec((1, RAD, A_OUT_WIDTH), lambda b, r, i: (b, jnp.maximum(i * blk - 1, 0), r * ngrp + group))
    nxt = pl.BlockSpec((1, RAD, A_OUT_WIDTH), lambda b, r, i: (b, jnp.minimum((i + 1) * blk, n_rad - 1), r * ngrp + group))
    out = pl.BlockSpec((1, tq, A_OUT_WIDTH), lambda b, r, i: (b, i, r))
    o, l = pl.pallas_call(
        functools.partial(_dilated_kernel, tq=tq, n_tiles=n_tiles),
        grid=(bn, dilation, n_tiles),
        in_specs=[cur, prev, cur, nxt, prev, cur, nxt, _resident(bias.shape)],
        out_specs=[out, out],
        out_shape=[jax.ShapeDtypeStruct((bn, sub_len, dilation * A_OUT_WIDTH), F32)] * 2,
        compiler_params=_params(("parallel", "parallel", "parallel")),
        name=f"dilated{group}",
    )(view(qa), view(ka), view(ka), view(ka), view(va), view(va), view(va), bias)
    return o.reshape(bn, s_len, A_OUT_WIDTH), l.reshape(bn, s_len, A_OUT_WIDTH)


NA_TILE_ROWS = 8
NA_TILE = NA_TILE_ROWS * GRID_W
NA_KEYS = NB_ROWS * GRID_W


def _na_bias(rpb):
    cols = np.arange(GRID_W)
    col_start = np.clip(cols - NB_COLS // 2, 0, GRID_W - NB_COLS)
    col_valid = (cols[None, :] >= col_start[:, None]) & (cols[None, :] < col_start[:, None] + NB_COLS)
    dc = np.clip(cols[None, :] - cols[:, None] + NB_COLS - 1, 0, 2 * NB_COLS - 2)
    rpb = rpb.astype(F32)
    variants = []
    for v in range(NB_ROWS):
        dr = np.clip(np.arange(NB_ROWS) + v, 0, 2 * NB_ROWS - 2)
        tab = rpb[:, dr][:, :, dc]
        tab = jnp.where(jnp.asarray(col_valid)[None, None], tab, NEG_INF)
        variants.append(jnp.transpose(tab, (0, 2, 1, 3)).reshape(B_HEADS, GRID_W, NA_KEYS))
    return jnp.stack(variants, axis=0)


def _na_kernel(q_ref, kp_ref, kc_ref, kn_ref, vp_ref, vc_ref, vn_ref, bias_ref, o_ref, kwin, vwin, *, rows):
    i = pl.program_id(1)
    kwin[0:NA_TILE] = kp_ref[0]
    kwin[NA_TILE:2 * NA_TILE] = kc_ref[0]
    kwin[2 * NA_TILE:3 * NA_TILE] = kn_ref[0]
    vwin[0:NA_TILE] = vp_ref[0]
    vwin[NA_TILE:2 * NA_TILE] = vc_ref[0]
    vwin[2 * NA_TILE:3 * NA_TILE] = vn_ref[0]
    lane = lax.broadcasted_iota(jnp.int32, (GRID_W, 2 * HEAD_DIM), 1)
    low = lane < HEAD_DIM
    for a in range(NA_TILE_ROWS):
        r = i * NA_TILE_ROWS + a
        rs = jnp.clip(r - NB_ROWS // 2, 0, rows - NB_ROWS)
        start = pl.multiple_of((rs - (i - 1) * NA_TILE_ROWS) * GRID_W, GRID_W)
        variant = rs - r + NB_ROWS - 1
        for hp in range(B_HEADS // 2):
            cs = slice(hp * 2 * HEAD_DIM, (hp + 1) * 2 * HEAD_DIM)
            q2 = q_ref[0, a * GRID_W:(a + 1) * GRID_W, cs]
            k2 = kwin[pl.ds(start, NA_KEYS), cs]
            v2 = vwin[pl.ds(start, NA_KEYS), cs]
            o2 = None
            for hh in range(2):
                mine = low if hh == 0 else jnp.logical_not(low)
                qh = jnp.where(mine, q2, jnp.zeros_like(q2))
                s = lax.dot_general(qh, k2, (((1,), (1,)), ((), ())), preferred_element_type=F32)
                s = s + bias_ref[variant, hp * 2 + hh]
                m = jnp.max(s, axis=-1, keepdims=True)
                p = jnp.exp(s - m)
                den = jnp.sum(p, axis=-1, keepdims=True)
                o = jnp.dot(p.astype(BF16), v2, preferred_element_type=F32) / den
                o2 = o if hh == 0 else jnp.where(low, o2, o)
            o_ref[0, a * GRID_W:(a + 1) * GRID_W, cs] = o2.astype(o_ref.dtype)


def _neighbourhood(qb, kb, vb, bias):
    bn, s_len, _ = qb.shape
    rows = s_len // GRID_W
    n_tiles = rows // NA_TILE_ROWS
    cur = pl.BlockSpec((1, NA_TILE, B_WIDTH), lambda b, i: (b, i, 0))
    prev = pl.BlockSpec((1, NA_TILE, B_WIDTH), lambda b, i: (b, jnp.maximum(i - 1, 0), 0))
    nxt = pl.BlockSpec((1, NA_TILE, B_WIDTH), lambda b, i: (b, jnp.minimum(i + 1, n_tiles - 1), 0))
    return pl.pallas_call(
        functools.partial(_na_kernel, rows=rows),
        grid=(bn, n_tiles),
        in_specs=[cur, prev, cur, nxt, prev, cur, nxt, _resident(bias.shape)],
        out_specs=cur,
        out_shape=jax.ShapeDtypeStruct((bn, s_len, B_WIDTH), BF16),
        scratch_shapes=[pltpu.VMEM((3 * NA_TILE, B_WIDTH), BF16)] * 2,
        compiler_params=_params(("parallel", "parallel")),
        name="neighbourhood",
    )(qb, kb, kb, kb, vb, vb, vb, bias)


def _merge_kernel(x_ref, o1_ref, l1_ref, o2_ref, l2_ref, o3_ref, l3_ref, ob_ref, ga_ref, gb_ref,
                  wpa_ref, wpb_ref, wout_ref, gffn_ref, wq_ref, h_ref, u_ref, qp_ref):
    l1, l2, l3 = l1_ref[...], l2_ref[...], l3_ref[...]
    m = jnp.maximum(jnp.maximum(l1, l2), l3)
    w1, w2, w3 = jnp.exp(l1 - m), jnp.exp(l2 - m), jnp.exp(l3 - m)
    oa = (w1 * o1_ref[...] + w2 * o2_ref[...] + w3 * o3_ref[...]) / (w1 + w2 + w3)
    pa = jnp.dot(oa.astype(BF16), wpa_ref[...], preferred_element_type=F32)
    pb = jnp.dot(ob_ref[...], wpb_ref[...], preferred_element_type=F32)
    merged = jax.nn.sigmoid(ga_ref[...].astype(F32)) * pa + jax.nn.sigmoid(gb_ref[...].astype(F32)) * pb
    h = x_ref[...] + jnp.dot(merged.astype(BF16), wout_ref[...], preferred_element_type=F32)
    u = _rms(h, gffn_ref[...])
    h_ref[...] = h
    u_ref[...] = u
    qp_ref[...] = jnp.dot(u.astype(BF16), wq_ref[...], preferred_element_type=F32).astype(qp_ref.dtype)


def _merge(x2, dil_outs, ob, ga, gb, wpa, wpb, wout, gffn, wq, tm):
    n_tok = x2.shape[0]
    tok = lambda w: pl.BlockSpec((tm, w), lambda i: (i, 0))
    ins = [x2]
    specs = [tok(D_MODEL)]
    for o, l in dil_outs:
        ins += [o, l]
        specs += [tok(A_OUT_WIDTH), tok(A_OUT_WIDTH)]
    ins += [ob, ga, gb, wpa, wpb, wout, gffn, wq]
    specs += [tok(B_WIDTH), tok(D_MODEL), tok(D_MODEL), _resident(wpa.shape), _resident(wpb.shape),
              _resident(wout.shape), _resident(gffn.shape), _resident(wq.shape)]
    qw = PEER_HEADS * PEER_QDIM
    return pl.pallas_call(
        _merge_kernel,
        grid=(n_tok // tm,),
        in_specs=specs,
        out_specs=[tok(D_MODEL), tok(D_MODEL), tok(qw)],
        out_shape=[jax.ShapeDtypeStruct((n_tok, D_MODEL), F32), jax.ShapeDtypeStruct((n_tok, D_MODEL), F32),
                   jax.ShapeDtypeStruct((n_tok, qw), BF16)],
        compiler_params=_params(("parallel",)),
        name="merge",
    )(*ins)


def _take_top(s, iota, count, payload=None):
    n = s.shape[0]
    vals, picks = [], []
    for _ in range(count):
        m = jnp.max(s, axis=0, keepdims=True)
        pos = jnp.min(jnp.where(s == m, iota, n), axis=0, keepdims=True)
        hit = iota == pos
        vals.append(m)
        picks.append(pos if payload is None else jnp.max(jnp.where(hit, payload, -1), axis=0, keepdims=True))
        s = jnp.where(hit, -jnp.inf, s)
    return vals, picks


def _topk_kernel(qp_ref, sk_ref, row_ref, sh_ref, gate_ref):
    tm = qp_ref.shape[0]
    iota = lax.broadcasted_iota(jnp.int32, (PEER_NKEYS, tm), 0)
    tops = []
    for p in range(2):
        q = qp_ref[:, p * PEER_HALF:(p + 1) * PEER_HALF]
        s = lax.dot_general(sk_ref[p], q, (((1,), (1,)), ((), ())), preferred_element_type=F32)
        tops.append(_take_top(s, iota, PEER_TOPK))
    (va, ia), (vb, ib) = tops
    vb_all = jnp.concatenate(vb, axis=0)
    ib_all = jnp.concatenate(ib, axis=0)
    cand = jnp.concatenate([va[a] + vb_all for a in range(PEER_TOPK)], axis=0)
    cidx = jnp.concatenate([ia[a] * PEER_NKEYS + ib_all for a in range(PEER_TOPK)], axis=0)
    iota2 = lax.broadcasted_iota(jnp.int32, (PEER_TOPK * PEER_TOPK, tm), 0)
    best, experts = _take_top(cand, iota2, PEER_TOPK, payload=cidx)
    best = jnp.concatenate(best, axis=0)
    e = jnp.exp(best - best[0:1])
    gate_ref[0] = e / jnp.sum(e, axis=0, keepdims=True)
    experts = jnp.concatenate(experts, axis=0)
    row_ref[0] = experts & (HALF_EXPERTS - 1)
    sh_ref[0] = jnp.where(experts < HALF_EXPERTS, 16, 0)


def _topk(qp, sk_bf16, tm):
    n_tok = qp.shape[0]
    n_tiles = n_tok // tm
    out = pl.BlockSpec((1, PEER_TOPK, tm), lambda i, h: (i, h, 0))
    shape = (n_tiles, PEER_SEL, tm)
    return pl.pallas_call(
        _topk_kernel,
        grid=(n_tiles, PEER_HEADS),
        in_specs=[pl.BlockSpec((tm, PEER_QDIM), lambda i, h: (i, h)), _resident(sk_bf16.shape)],
        out_specs=[out, out, out],
        out_shape=[jax.ShapeDtypeStruct(shape, jnp.int32), jax.ShapeDtypeStruct(shape, jnp.int32),
                   jax.ShapeDtypeStruct(shape, F32)],
        compiler_params=_params(("parallel", "parallel")),
        name="peer_topk",
    )(qp, sk_bf16)


HIGH_HALF = -65536


def _pack_experts(tbl):
    b = lax.bitcast_convert_type(tbl.astype(BF16), jnp.uint16).astype(jnp.uint32)
    w = (b[HALF_EXPERTS:] << 16) | b[:HALF_EXPERTS]
    return lax.bitcast_convert_type(w, jnp.int32).reshape(HALF_EXPERTS, SUBLANES, LANES)


def _expert_row(tbl_ref, row, shift):
    w = tbl_ref[row]
    return lax.bitcast_convert_type((w << shift) & HIGH_HALF, F32)


def _sublane_sums(parts, masks):
    shift = SUBLANES // 2
    for mask in masks:
        half = len(parts) // 2
        nxt = []
        for j in range(half):
            a, b = parts[j], parts[j + half]
            nxt.append(jnp.where(mask, a, pltpu.roll(b, shift, 0)) + jnp.where(mask, pltpu.roll(a, SUBLANES - shift, 0), b))
        parts = nxt
        shift //= 2
    return parts[0]


PEER_GROUP = SUBLANES
GROUPS_PER_TOKEN = PEER_SEL // PEER_GROUP


def _peer_u_kernel(row_sm, sh_sm, x_ref, gate_ref, tbl_ref, coef_ref, sums_ref):
    tm = x_ref.shape[0]
    sub = lax.broadcasted_iota(jnp.int32, (SUBLANES, LANES), 0)
    masks = (sub < 4, (sub & 3) < 2, (sub & 1) == 0)
    lane = lax.broadcasted_iota(jnp.int32, (PEER_SEL, tm), 1)

    def token(t, carry):
        x = x_ref[t]
        at = t * PEER_SEL
        for g in range(GROUPS_PER_TOKEN):
            prods = [_expert_row(tbl_ref, row_sm[at + g * PEER_GROUP + j], sh_sm[at + g * PEER_GROUP + j]) * x
                     for j in range(PEER_GROUP)]
            sums_ref[pl.ds(pl.multiple_of(at + g * SUBLANES, SUBLANES), SUBLANES), :] = _sublane_sums(prods, masks)
        return carry

    lax.fori_loop(0, tm, token, 0)

    def finish(i, act):
        for j in range(SUBLANES):
            t = i * SUBLANES + j
            rows = sums_ref[pl.ds(pl.multiple_of(t * PEER_SEL, PEER_SEL), PEER_SEL), :]
            dots = jnp.sum(rows, axis=-1, keepdims=True)
            act = jnp.where(lane == t, dots, act)
        return act

    act = lax.fori_loop(0, tm // SUBLANES, finish, jnp.zeros((PEER_SEL, tm), F32))
    coef_ref[0] = gate_ref[0] * jax.nn.gelu(act)


PACKED_ROWS = 2 * SUBLANES
V_CHUNK = 32
V_CHUNK_COLS = V_CHUNK * PACKED_ROWS
V_COLS = PEER_SEL * PACKED_ROWS
V_TOKENS_PER_STEP = 4


def _v_expand_matrices():
    col = np.arange(V_COLS)
    src = np.arange(2 * PEER_SEL)
    e = (src[:, None] // 2 == col[None, :] // PACKED_ROWS) & (src[:, None] % 2 == col[None, :] % 2)
    d = (col[None, :] % PACKED_ROWS) // 2 == np.arange(SUBLANES)[:, None]
    return jnp.asarray(e, BF16), jnp.asarray(d, F32)


def _peer_v_kernel(row_sm, c2_ref, h_ref, g_ref, e_ref, d_ref, tbl_ref, y_ref, ce_ref):
    tm = h_ref.shape[0]
    c2 = c2_ref[...]
    c_hi = c2.astype(BF16)
    c_lo = (c2 - c_hi.astype(F32)).astype(BF16)
    ce_ref[0] = jnp.dot(c_hi, e_ref[...], preferred_element_type=F32)
    ce_ref[1] = jnp.dot(c_lo, e_ref[...], preferred_element_type=F32)
    diag = d_ref[...]
    half_cols = V_CHUNK_COLS // 2

    def token(t):
        lhs = jnp.concatenate([ce_ref[0, pl.ds(t, 1), :] * diag, ce_ref[1, pl.ds(t, 1), :] * diag],
                              axis=0).astype(BF16)
        at = t * PEER_SEL
        acc = None
        for c in range(PEER_SEL // V_CHUNK):
            tiles = [pltpu.bitcast(tbl_ref[row_sm[at + c * V_CHUNK + j]], BF16) for j in range(V_CHUNK)]
            w = jnp.concatenate([jnp.concatenate(tiles[:V_CHUNK // 2], axis=0),
                                 jnp.concatenate(tiles[V_CHUNK // 2:], axis=0)], axis=1)
            c0 = c * V_CHUNK_COLS
            left = jnp.concatenate([lhs[:, c0:c0 + half_cols], lhs[:, c0 + half_cols:c0 + V_CHUNK_COLS]], axis=0)
            out = jnp.dot(left, w, preferred_element_type=F32)
            acc = out if acc is None else acc + out
        first = acc[0:SUBLANES, 0:LANES] + acc[SUBLANES:2 * SUBLANES, 0:LANES]
        second = acc[2 * SUBLANES:3 * SUBLANES, LANES:] + acc[3 * SUBLANES:, LANES:]
        y_ref[t] = h_ref[t] + (first + second)

    def tokens(i, carry):
        for j in range(V_TOKENS_PER_STEP):
            token(i * V_TOKENS_PER_STEP + j)
        return carry

    lax.fori_loop(0, tm // V_TOKENS_PER_STEP, tokens, 0)
    h = y_ref[...]
    ms = jnp.sum(jnp.sum(h * h, axis=2, keepdims=True), axis=1, keepdims=True) * (1.0 / D_MODEL)
    y_ref[...] = h * lax.rsqrt(ms + NORM_EPS) * g_ref[...]


def _smem_tile(tm):
    return pl.BlockSpec((tm * PEER_SEL,), lambda i: (i,), memory_space=pltpu.SMEM)


def _token_major(t):
    return jnp.swapaxes(t, 1, 2).reshape(-1)


def _peer_u(rows, shifts, gates, u3, tbl, tm):
    n_tiles = gates.shape[0]
    vm = pl.BlockSpec((1, PEER_SEL, tm), lambda i: (i, 0, 0))
    return pl.pallas_call(
        _peer_u_kernel,
        grid=(n_tiles,),
        in_specs=[_smem_tile(tm), _smem_tile(tm), pl.BlockSpec((tm, SUBLANES, LANES), lambda i: (i, 0, 0)), vm,
                  _resident(tbl.shape)],
        out_specs=vm,
        out_shape=jax.ShapeDtypeStruct((n_tiles, PEER_SEL, tm), F32),
        scratch_shapes=[pltpu.VMEM((tm * PEER_SEL, LANES), F32)],
        compiler_params=_params(("arbitrary",)),
        name="peer_u",
    )(rows, shifts, u3, gates, tbl)


def _peer_v(rows, coef_pairs, h3, g_final, tbl, tm):
    n_tiles = h3.shape[0] // tm
    tok = pl.BlockSpec((tm, SUBLANES, LANES), lambda i: (i, 0, 0))
    expand, diag = _v_expand_matrices()
    return pl.pallas_call(
        _peer_v_kernel,
        grid=(n_tiles,),
        in_specs=[_smem_tile(tm), pl.BlockSpec((tm, 2 * PEER_SEL), lambda i: (i, 0)), tok,
                  _resident(g_final.shape), _resident(expand.shape), _resident(diag.shape), _resident(tbl.shape)],
        out_specs=tok,
        out_shape=jax.ShapeDtypeStruct(h3.shape, F32),
        scratch_shapes=[pltpu.VMEM((2, tm, V_COLS), F32)],
        compiler_params=_params(("arbitrary",)),
        name="peer_v",
    )(rows, coef_pairs, h3, g_final, expand, diag, tbl)


PROJ_TILE = 512
MERGE_TILE = 256
PEER_TILE = 128


def _trunk(x, p):
    bn, s_len, d = x.shape
    n_tok = bn * s_len
    x2 = x.reshape(n_tok, d)
    qa, ka, va, qb, kb, vb, ga, gb = _inproj(x2, p["norm_mix"], p["w_in"], PROJ_TILE)
    seq = lambda t: t.reshape(bn, s_len, t.shape[-1])
    dil_outs = []
    for group, (window, dilation) in enumerate(DIL_GROUPS):
        assert window // (2 * dilation) == RAD
        sub_len = s_len // dilation
        assert s_len % dilation == 0 and sub_len % RAD == 0
        tq = min(256, sub_len)
        assert sub_len % tq == 0
        bias = _dilated_bias(p["t5_table"], group, dilation, tq)
        o, l = _dilated_group(seq(qa), seq(ka), seq(va), bias, group, dilation, tq)
        dil_outs.append((o.reshape(n_tok, A_OUT_WIDTH), l.reshape(n_tok, A_OUT_WIDTH)))
    assert s_len % NA_TILE == 0 and s_len // GRID_W >= NB_ROWS
    ob = _neighbourhood(seq(qb), seq(kb), seq(vb), p["na_bias"]).reshape(n_tok, B_WIDTH)
    h, u, qp = _merge(x2, dil_outs, ob, ga, gb, p["w_proj_a"], p["w_proj_b"], p["w_out"], p["norm_ffn"],
                      p["w_query"], MERGE_TILE)
    rows, shifts, gates = _topk(qp, p["sub_keys"], PEER_TILE)
    tile3 = lambda t: t.reshape(n_tok, SUBLANES, LANES)
    rows, shifts = _token_major(rows), _token_major(shifts)
    coefs = _peer_u(rows, shifts, gates, tile3(u), p["expert_u"], PEER_TILE)
    coefs = _token_major(coefs)
    coef_pairs = jnp.stack([jnp.where(shifts == 16, coefs, 0.0), jnp.where(shifts == 16, 0.0, coefs)], axis=-1)
    y = _peer_v(rows, coef_pairs.reshape(n_tok, 2 * PEER_SEL), tile3(h), p["norm_final"], p["expert_v"], PEER_TILE)
    return y.reshape(bn, s_len, d)


def kernel(x_prompt, x_sample, norm_mix, w_in, w_proj_a, w_proj_b, w_out, rpb, norm_ffn, w_query, sub_keys,
           expert_u, expert_v, t5_table, norm_final):
    assert norm_mix.shape[0] == 1, "single-layer model"
    p = {
        "norm_mix": norm_mix[0].reshape(1, D_MODEL).astype(F32),
        "w_in": w_in[0].astype(BF16),
        "w_proj_a": w_proj_a[0].astype(BF16),
        "w_proj_b": w_proj_b[0].astype(BF16),
        "w_out": w_out[0].astype(BF16),
        "na_bias": _na_bias(rpb[0]),
        "norm_ffn": norm_ffn[0].reshape(1, D_MODEL).astype(F32),
        "w_query": w_query[0].astype(BF16),
        "sub_keys": sub_keys[0].astype(BF16),
        "expert_u": _pack_experts(expert_u[0]),
        "expert_v": _pack_experts(expert_v[0]),
        "t5_table": t5_table,
        "norm_final": norm_final.reshape(SUBLANES, LANES).astype(F32),
    }
    return (_trunk(x_prompt, p), _trunk(x_sample, p))
```

```python
import functools
import math

import numpy as np
import jax
import jax.numpy as jnp
from jax import lax
from jax.experimental import pallas as pl
from jax.experimental.pallas import tpu as pltpu

D_MODEL = 1024
HEAD_DIM = 64
DIL_GROUPS = ((128, 1), (512, 4), (2048, 16))
A_HEADS_PER_GROUP = 4
A_HEADS = A_HEADS_PER_GROUP * len(DIL_GROUPS)
A_WIDTH = A_HEADS * HEAD_DIM
A_OUT_WIDTH = A_HEADS_PER_GROUP * HEAD_DIM
T5_BUCKETS = 32
T5_MAX_DIST = 1024
GRID_W = 64
B_HEADS = 8
B_WIDTH = B_HEADS * HEAD_DIM
NB_ROWS = 8
NB_COLS = 16
PEER_HEADS = 8
PEER_NKEYS = 128
PEER_EXPERTS = PEER_NKEYS * PEER_NKEYS
PEER_QDIM = 256
PEER_HALF = PEER_QDIM // 2
PEER_TOPK = 16
PEER_SEL = PEER_HEADS * PEER_TOPK
NORM_EPS = 1e-6
NEG_INF = -1e30

RAD = 64
SUBLANES = 8
LANES = 128
HALF_EXPERTS = PEER_EXPERTS // 2
VMEM_LIMIT = 48 * 1024 * 1024

F32 = jnp.float32
BF16 = jnp.bfloat16


def _resident(shape):
    nd = len(shape)
    return pl.BlockSpec(shape, lambda *_: (0,) * nd, pipeline_mode=pl.Buffered(1))


def _params(sem):
    return pltpu.CompilerParams(dimension_semantics=sem, vmem_limit_bytes=VMEM_LIMIT)


IN_SPLITS = (A_WIDTH, A_WIDTH, A_WIDTH, B_WIDTH, B_WIDTH, B_WIDTH, D_MODEL, D_MODEL)
IN_SCALES = (HEAD_DIM ** -0.5, 1.0, 1.0, HEAD_DIM ** -0.5, 1.0, 1.0, 1.0, 1.0)


def _rms(x, g):
    return x * lax.rsqrt(jnp.mean(x * x, axis=-1, keepdims=True) + NORM_EPS) * g


def _inproj_kernel(x_ref, g_ref, w_ref, *out_refs):
    u = _rms(x_ref[...], g_ref[...]).astype(BF16)
    off = 0
    for o_ref, width, scale in zip(out_refs, IN_SPLITS, IN_SCALES):
        z = jnp.dot(u, w_ref[:, off:off + width], preferred_element_type=F32)
        if scale != 1.0:
            z = z * scale
        o_ref[...] = z.astype(o_ref.dtype)
        off += width


def _inproj(x2, g, w_bf16, tm):
    n_tok = x2.shape[0]
    return pl.pallas_call(
        _inproj_kernel,
        grid=(n_tok // tm,),
        in_specs=[pl.BlockSpec((tm, D_MODEL), lambda i: (i, 0)),
                  _resident((1, D_MODEL)),
                  _resident(w_bf16.shape)],
        out_specs=[pl.BlockSpec((tm, w), lambda i: (i, 0)) for w in IN_SPLITS],
        out_shape=[jax.ShapeDtypeStruct((n_tok, w), BF16) for w in IN_SPLITS],
        compiler_params=_params(("parallel",)),
        name="inproj",
    )(x2, g, w_bf16)


def _t5_bucket(rel):
    n = -rel
    nb = T5_BUCKETS // 2
    ret = (n < 0).astype(np.int32) * nb
    n = np.abs(n)
    max_exact = nb // 2
    large = max_exact + (np.log(np.maximum(n, 1) / max_exact) / math.log(T5_MAX_DIST / max_exact)
                         * (nb - max_exact)).astype(np.int32)
    large = np.minimum(large, nb - 1)
    return (ret + np.where(n < max_exact, n, large)).astype(np.int32)


def _dilated_bias(t5_table, group, dilation, tq):
    rel = np.arange(tq + 2 * RAD)[None, :] - RAD - np.arange(tq)[:, None]
    tab = t5_table[:, group * A_HEADS_PER_GROUP:(group + 1) * A_HEADS_PER_GROUP].astype(F32)
    in_band = np.abs(rel) <= RAD
    bucket = np.where(in_band, _t5_bucket(rel * dilation), -1)
    bias = jnp.full((A_HEADS_PER_GROUP,) + rel.shape, NEG_INF, F32)
    for b in np.unique(bucket[in_band]):
        bias = jnp.where(jnp.asarray(bucket == b)[None], tab[int(b)][:, None, None], bias)
    return bias


def _dilated_kernel(q_ref, kp_ref, kc_ref, kn_ref, vp_ref, vc_ref, vn_ref, bias_ref, o_ref, l_ref, *, tq, n_tiles):
    i = pl.program_id(2)
    q = q_ref[0]
    k_all = jnp.concatenate([kp_ref[0], kc_ref[0], kn_ref[0]], axis=0)
    v_all = jnp.concatenate([vp_ref[0], vc_ref[0], vn_ref[0]], axis=0)
    nk = tq + 2 * RAD
    col = lax.broadcasted_iota(jnp.int32, (tq, nk), 1)
    first_valid = jnp.where(i == 0, RAD, 0)
    end_valid = jnp.where(i == n_tiles - 1, tq + RAD, nk)
    valid = (col >= first_valid) & (col < end_valid)
    lane = lax.broadcasted_iota(jnp.int32, (tq, A_OUT_WIDTH), 1)
    o_acc = jnp.zeros((tq, A_OUT_WIDTH), F32)
    l_acc = jnp.zeros((tq, A_OUT_WIDTH), F32)
    for h in range(A_HEADS_PER_GROUP):
        head = (lane >= h * HEAD_DIM) & (lane < (h + 1) * HEAD_DIM)
        qh = jnp.where(head, q, jnp.zeros_like(q))
        s = lax.dot_general(qh, k_all, (((1,), (1,)), ((), ())), preferred_element_type=F32)
        s = jnp.where(valid, s + bias_ref[h], NEG_INF)
        m = jnp.max(s, axis=-1, keepdims=True)
        p = jnp.exp(s - m)
        den = jnp.sum(p, axis=-1, keepdims=True)
        o = jnp.dot(p.astype(BF16), v_all, preferred_element_type=F32) / den
        lse = m + jnp.log(den)
        o_acc = jnp.where(head, o, o_acc)
        l_acc = jnp.where(head, lse, l_acc)
    o_ref[0] = o_acc
    l_ref[0] = l_acc


def _dilated_group(qa, ka, va, bias, group, dilation, tq):
    bn, s_len, _ = qa.shape
    sub_len = s_len // dilation
    n_tiles = sub_len // tq
    blk = tq // RAD
    n_rad = sub_len // RAD
    ngrp = len(DIL_GROUPS)
    view = lambda t: t.reshape(bn, sub_len, dilation * A_WIDTH)
    cur = pl.BlockSpec((1, tq, A_OUT_WIDTH), lambda b, r, i: (b, i, r * ngrp + group))
    prev = pl.BlockSpec((1, RAD, A_OUT_WIDTH), lambda b, r, i: (b, jnp.maximum(i * blk - 1, 0), r * ngrp + group))
    nxt = pl.BlockSpec((1, RAD, A_OUT_WIDTH), lambda b, r, i: (b, jnp.minimum((i + 1) * blk, n_rad - 1), r * ngrp + group))
    out = pl.BlockSpec((1, tq, A_OUT_WIDTH), lambda b, r, i: (b, i, r))
    o, l = pl.pallas_call(
        functools.partial(_dilated_kernel, tq=tq, n_tiles=n_tiles),
        grid=(bn, dilation, n_tiles),
        in_specs=[cur, prev, cur, nxt, prev, cur, nxt, _resident(bias.shape)],
        out_specs=[out, out],
        out_shape=[jax.ShapeDtypeStruct((bn, sub_len, dilation * A_OUT_WIDTH), F32)] * 2,
        compiler_params=_params(("parallel", "parallel", "parallel")),
        name=f"dilated{group}",
    )(view(qa), view(ka), view(ka), view(ka), view(va), view(va), view(va), bias)
    return o.reshape(bn, s_len, A_OUT_WIDTH), l.reshape(bn, s_len, A_OUT_WIDTH)


NA_TILE_ROWS = 8
NA_TILE = NA_TILE_ROWS * GRID_W
NA_KEYS = NB_ROWS * GRID_W


def _na_bias(rpb):
    cols = np.arange(GRID_W)
    col_start = np.clip(cols - NB_COLS // 2, 0, GRID_W - NB_COLS)
    col_valid = (cols[None, :] >= col_start[:, None]) & (cols[None, :] < col_start[:, None] + NB_COLS)
    dc = np.clip(cols[None, :] - cols[:, None] + NB_COLS - 1, 0, 2 * NB_COLS - 2)
    rpb = rpb.astype(F32)
    variants = []
    for v in range(NB_ROWS):
        dr = np.clip(np.arange(NB_ROWS) + v, 0, 2 * NB_ROWS - 2)
        tab = rpb[:, dr][:, :, dc]
        tab = jnp.where(jnp.asarray(col_valid)[None, None], tab, NEG_INF)
        variants.append(jnp.transpose(tab, (0, 2, 1, 3)).reshape(B_HEADS, GRID_W, NA_KEYS))
    return jnp.stack(variants, axis=0)


def _na_kernel(q_ref, kp_ref, kc_ref, kn_ref, vp_ref, vc_ref, vn_ref, bias_ref, o_ref, kwin, vwin, *, rows):
    i = pl.program_id(1)
    kwin[0:NA_TILE] = kp_ref[0]
    kwin[NA_TILE:2 * NA_TILE] = kc_ref[0]
    kwin[2 * NA_TILE:3 * NA_TILE] = kn_ref[0]
    vwin[0:NA_TILE] = vp_ref[0]
    vwin[NA_TILE:2 * NA_TILE] = vc_ref[0]
    vwin[2 * NA_TILE:3 * NA_TILE] = vn_ref[0]
    lane = lax.broadcasted_iota(jnp.int32, (GRID_W, 2 * HEAD_DIM), 1)
    low = lane < HEAD_DIM
    for a in range(NA_TILE_ROWS):
        r = i * NA_TILE_ROWS + a
        rs = jnp.clip(r - NB_ROWS // 2, 0, rows - NB_ROWS)
        start = pl.multiple_of((rs - (i - 1) * NA_TILE_ROWS) * GRID_W, GRID_W)
        variant = rs - r + NB_ROWS - 1
        for hp in range(B_HEADS // 2):
            cs = slice(hp * 2 * HEAD_DIM, (hp + 1) * 2 * HEAD_DIM)
            q2 = q_ref[0, a * GRID_W:(a + 1) * GRID_W, cs]
            k2 = kwin[pl.ds(start, NA_KEYS), cs]
            v2 = vwin[pl.ds(start, NA_KEYS), cs]
            o2 = None
            for hh in range(2):
                mine = low if hh == 0 else jnp.logical_not(low)
                qh = jnp.where(mine, q2, jnp.zeros_like(q2))
                s = lax.dot_general(qh, k2, (((1,), (1,)), ((), ())), preferred_element_type=F32)
                s = s + bias_ref[variant, hp * 2 + hh]
                m = jnp.max(s, axis=-1, keepdims=True)
                p = jnp.exp(s - m)
                den = jnp.sum(p, axis=-1, keepdims=True)
                o = jnp.dot(p.astype(BF16), v2, preferred_element_type=F32) / den
                o2 = o if hh == 0 else jnp.where(low, o2, o)
            o_ref[0, a * GRID_W:(a + 1) * GRID_W, cs] = o2.astype(o_ref.dtype)


def _neighbourhood(qb, kb, vb, bias):
    bn, s_len, _ = qb.shape
    rows = s_len // GRID_W
    n_tiles = rows // NA_TILE_ROWS
    cur = pl.BlockSpec((1, NA_TILE, B_WIDTH), lambda b, i: (b, i, 0))
    prev = pl.BlockSpec((1, NA_TILE, B_WIDTH), lambda b, i: (b, jnp.maximum(i - 1, 0), 0))
    nxt = pl.BlockSpec((1, NA_TILE, B_WIDTH), lambda b, i: (b, jnp.minimum(i + 1, n_tiles - 1), 0))
    return pl.pallas_call(
        functools.partial(_na_kernel, rows=rows),
        grid=(bn, n_tiles),
        in_specs=[cur, prev, cur, nxt, prev, cur, nxt, _resident(bias.shape)],
        out_specs=cur,
        out_shape=jax.ShapeDtypeStruct((bn, s_len, B_WIDTH), BF16),
        scratch_shapes=[pltpu.VMEM((3 * NA_TILE, B_WIDTH), BF16)] * 2,
        compiler_params=_params(("parallel", "parallel")),
        name="neighbourhood",
    )(qb, kb, kb, kb, vb, vb, vb, bias)


def _merge_kernel(x_ref, o1_ref, l1_ref, o2_ref, l2_ref, o3_ref, l3_ref, ob_ref, ga_ref, gb_ref,
                  wpa_ref, wpb_ref, wout_ref, gffn_ref, wq_ref, h_ref, u_ref, qp_ref):
    l1, l2, l3 = l1_ref[...], l2_ref[...], l3_ref[...]
    m = jnp.maximum(jnp.maximum(l1, l2), l3)
    w1, w2, w3 = jnp.exp(l1 - m), jnp.exp(l2 - m), jnp.exp(l3 - m)
    oa = (w1 * o1_ref[...] + w2 * o2_ref[...] + w3 * o3_ref[...]) / (w1 + w2 + w3)
    pa = jnp.dot(oa.astype(BF16), wpa_ref[...], preferred_element_type=F32)
    pb = jnp.dot(ob_ref[...], wpb_ref[...], preferred_element_type=F32)
    merged = jax.nn.sigmoid(ga_ref[...].astype(F32)) * pa + jax.nn.sigmoid(gb_ref[...].astype(F32)) * pb
    h = x_ref[...] + jnp.dot(merged.astype(BF16), wout_ref[...], preferred_element_type=F32)
    u = _rms(h, gffn_ref[...])
    h_ref[...] = h
    u_ref[...] = u
    qp_ref[...] = jnp.dot(u.astype(BF16), wq_ref[...], preferred_element_type=F32).astype(qp_ref.dtype)


def _merge(x2, dil_outs, ob, ga, gb, wpa, wpb, wout, gffn, wq, tm):
    n_tok = x2.shape[0]
    tok = lambda w: pl.BlockSpec((tm, w), lambda i: (i, 0))
    ins = [x2]
    specs = [tok(D_MODEL)]
    for o, l in dil_outs:
        ins += [o, l]
        specs += [tok(A_OUT_WIDTH), tok(A_OUT_WIDTH)]
    ins += [ob, ga, gb, wpa, wpb, wout, gffn, wq]
    specs += [tok(B_WIDTH), tok(D_MODEL), tok(D_MODEL), _resident(wpa.shape), _resident(wpb.shape),
              _resident(wout.shape), _resident(gffn.shape), _resident(wq.shape)]
    qw = PEER_HEADS * PEER_QDIM
    return pl.pallas_call(
        _merge_kernel,
        grid=(n_tok // tm,),
        in_specs=specs,
        out_specs=[tok(D_MODEL), tok(D_MODEL), tok(qw)],
        out_shape=[jax.ShapeDtypeStruct((n_tok, D_MODEL), F32), jax.ShapeDtypeStruct((n_tok, D_MODEL), F32),
                   jax.ShapeDtypeStruct((n_tok, qw), BF16)],
        compiler_params=_params(("parallel",)),
        name="merge",
    )(*ins)


def _take_top(s, iota, count, payload=None):
    n = s.shape[0]
    vals, picks = [], []
    for _ in range(count):
        m = jnp.max(s, axis=0, keepdims=True)
        pos = jnp.min(jnp.where(s == m, iota, n), axis=0, keepdims=True)
        hit = iota == pos
        vals.append(m)
        picks.append(pos if payload is None else jnp.max(jnp.where(hit, payload, -1), axis=0, keepdims=True))
        s = jnp.where(hit, -jnp.inf, s)
    return vals, picks


def _topk_kernel(qp_ref, sk_ref, row_tm_ref, sh_tm_ref, sh_ref, gate_ref, row_scr, sh_scr):
    tm = qp_ref.shape[0]
    head = pl.program_id(1)
    iota = lax.broadcasted_iota(jnp.int32, (PEER_NKEYS, tm), 0)
    tops = []
    for p in range(2):
        q = qp_ref[:, p * PEER_HALF:(p + 1) * PEER_HALF]
        s = lax.dot_general(sk_ref[p], q, (((1,), (1,)), ((), ())), preferred_element_type=F32)
        tops.append(_take_top(s, iota, PEER_TOPK))
    (va, ia), (vb, ib) = tops
    vb_all = jnp.concatenate(vb, axis=0)
    ib_all = jnp.concatenate(ib, axis=0)
    cand = jnp.concatenate([va[a] + vb_all for a in range(PEER_TOPK)], axis=0)
    cidx = jnp.concatenate([ia[a] * PEER_NKEYS + ib_all for a in range(PEER_TOPK)], axis=0)
    iota2 = lax.broadcasted_iota(jnp.int32, (PEER_TOPK * PEER_TOPK, tm), 0)
    best, experts = _take_top(cand, iota2, PEER_TOPK, payload=cidx)
    best = jnp.concatenate(best, axis=0)
    e = jnp.exp(best - best[0:1])
    gate_ref[0] = e / jnp.sum(e, axis=0, keepdims=True)
    experts = jnp.concatenate(experts, axis=0)
    shift = jnp.where(experts < HALF_EXPERTS, 16, 0)
    sh_ref[0] = shift
    slots = pl.ds(pl.multiple_of(head * PEER_TOPK, PEER_TOPK), PEER_TOPK)
    row_scr[slots, :] = experts & (HALF_EXPERTS - 1)
    sh_scr[slots, :] = shift

    @pl.when(head == PEER_HEADS - 1)
    def _():
        row_tm_ref[0] = row_scr[...].T
        sh_tm_ref[0] = sh_scr[...].T


def _topk(qp, sk_bf16, tm):
    n_tok = qp.shape[0]
    n_tiles = n_tok // tm
    per_head = pl.BlockSpec((1, PEER_TOPK, tm), lambda i, h: (i, h, 0))
    per_tile = pl.BlockSpec((1, tm, PEER_SEL), lambda i, h: (i, 0, 0))
    slot_major = (n_tiles, PEER_SEL, tm)
    token_major = (n_tiles, tm, PEER_SEL)
    return pl.pallas_call(
        _topk_kernel,
        grid=(n_tiles, PEER_HEADS),
        in_specs=[pl.BlockSpec((tm, PEER_QDIM), lambda i, h: (i, h)), _resident(sk_bf16.shape)],
        out_specs=[per_tile, per_tile, per_head, per_head],
        out_shape=[jax.ShapeDtypeStruct(token_major, jnp.int32), jax.ShapeDtypeStruct(token_major, jnp.int32),
                   jax.ShapeDtypeStruct(slot_major, jnp.int32), jax.ShapeDtypeStruct(slot_major, F32)],
        scratch_shapes=[pltpu.VMEM((PEER_SEL, tm), jnp.int32)] * 2,
        compiler_params=_params(("parallel", "arbitrary")),
        name="peer_topk",
    )(qp, sk_bf16)


HIGH_HALF = -65536


def _pack_experts(tbl):
    b = lax.bitcast_convert_type(tbl.astype(BF16), jnp.uint16).astype(jnp.uint32)
    w = (b[HALF_EXPERTS:] << 16) | b[:HALF_EXPERTS]
    return lax.bitcast_convert_type(w, jnp.int32).reshape(HALF_EXPERTS, SUBLANES, LANES)


def _expert_row(tbl_ref, row, shift):
    w = tbl_ref[row]
    return lax.bitcast_convert_type((w << shift) & HIGH_HALF, F32)


def _sublane_sums(parts, masks):
    shift = SUBLANES // 2
    for mask in masks:
        half = len(parts) // 2
        nxt = []
        for j in range(half):
            a, b = parts[j], parts[j + half]
            nxt.append(jnp.where(mask, a, pltpu.roll(b, shift, 0)) + jnp.where(mask, pltpu.roll(a, SUBLANES - shift, 0), b))
        parts = nxt
        shift //= 2
    return parts[0]


PEER_GROUP = SUBLANES
GROUPS_PER_TOKEN = PEER_SEL // PEER_GROUP


def _peer_u_kernel(row_sm, sh_sm, x_ref, gate_ref, sh_ref, tbl_ref, pair_ref, sums_ref):
    tm = x_ref.shape[0]
    sub = lax.broadcasted_iota(jnp.int32, (SUBLANES, LANES), 0)
    masks = (sub < 4, (sub & 3) < 2, (sub & 1) == 0)
    lane = lax.broadcasted_iota(jnp.int32, (PEER_SEL, tm), 1)

    def token(t, carry):
        x = x_ref[t]
        for g in range(GROUPS_PER_TOKEN):
            prods = [_expert_row(tbl_ref, row_sm[0, t, g * PEER_GROUP + j], sh_sm[0, t, g * PEER_GROUP + j]) * x
                     for j in range(PEER_GROUP)]
            at = pl.multiple_of(t * PEER_SEL + g * SUBLANES, SUBLANES)
            sums_ref[pl.ds(at, SUBLANES), :] = _sublane_sums(prods, masks)
        return carry

    lax.fori_loop(0, tm, token, 0)

    def finish(i, act):
        for j in range(SUBLANES):
            t = i * SUBLANES + j
            rows = sums_ref[pl.ds(pl.multiple_of(t * PEER_SEL, PEER_SEL), PEER_SEL), :]
            dots = jnp.sum(rows, axis=-1, keepdims=True)
            act = jnp.where(lane == t, dots, act)
        return act

    act = lax.fori_loop(0, tm // SUBLANES, finish, jnp.zeros((PEER_SEL, tm), F32))
    coef = gate_ref[0] * jax.nn.gelu(act)
    low = sh_ref[0] == 16
    pair_ref[:, 0:PEER_SEL] = jnp.where(low, coef, 0.0).T
    pair_ref[:, PEER_SEL:] = jnp.where(low, 0.0, coef).T


PACKED_ROWS = 2 * SUBLANES
V_CHUNK = 32
V_CHUNK_COLS = V_CHUNK * PACKED_ROWS
V_COLS = PEER_SEL * PACKED_ROWS
V_TOKENS_PER_STEP = 4


def _v_expand_matrices():
    col = np.arange(V_COLS)
    src = np.arange(2 * PEER_SEL)
    e = (src[:, None] % PEER_SEL == col[None, :] // PACKED_ROWS) & (src[:, None] // PEER_SEL == col[None, :] % 2)
    d = (col[None, :] % PACKED_ROWS) // 2 == np.arange(SUBLANES)[:, None]
    return jnp.asarray(e, BF16), jnp.asarray(d, F32)


def _peer_v_kernel(row_sm, c2_ref, h_ref, g_ref, e_ref, d_ref, tbl_ref, y_ref, ce_ref):
    tm = h_ref.shape[0]
    c2 = c2_ref[...]
    c_hi = c2.astype(BF16)
    c_lo = (c2 - c_hi.astype(F32)).astype(BF16)
    ce_ref[0] = jnp.dot(c_hi, e_ref[...], preferred_element_type=F32)
    ce_ref[1] = jnp.dot(c_lo, e_ref[...], preferred_element_type=F32)
    diag = d_ref[...]
    half_cols = V_CHUNK_COLS // 2

    def token(t):
        lhs = jnp.concatenate([ce_ref[0, pl.ds(t, 1), :] * diag, ce_ref[1, pl.ds(t, 1), :] * diag],
                              axis=0).astype(BF16)
        acc = None
        for c in range(PEER_SEL // V_CHUNK):
            tiles = [pltpu.bitcast(tbl_ref[row_sm[0, t, c * V_CHUNK + j]], BF16) for j in range(V_CHUNK)]
            w = jnp.concatenate([jnp.concatenate(tiles[:V_CHUNK // 2], axis=0),
                                 jnp.concatenate(tiles[V_CHUNK // 2:], axis=0)], axis=1)
            c0 = c * V_CHUNK_COLS
            left = jnp.concatenate([lhs[:, c0:c0 + half_cols], lhs[:, c0 + half_cols:c0 + V_CHUNK_COLS]], axis=0)
            out = jnp.dot(left, w, preferred_element_type=F32)
            acc = out if acc is None else acc + out
        first = acc[0:SUBLANES, 0:LANES] + acc[SUBLANES:2 * SUBLANES, 0:LANES]
        second = acc[2 * SUBLANES:3 * SUBLANES, LANES:] + acc[3 * SUBLANES:, LANES:]
        y_ref[t] = h_ref[t] + (first + second)

    def tokens(i, carry):
        for j in range(V_TOKENS_PER_STEP):
            token(i * V_TOKENS_PER_STEP + j)
        return carry

    lax.fori_loop(0, tm // V_TOKENS_PER_STEP, tokens, 0)
    h = y_ref[...]
    ms = jnp.sum(jnp.sum(h * h, axis=2, keepdims=True), axis=1, keepdims=True) * (1.0 / D_MODEL)
    y_ref[...] = h * lax.rsqrt(ms + NORM_EPS) * g_ref[...]


def _smem_tile(tm):
    return pl.BlockSpec((1, tm, PEER_SEL), lambda i: (i, 0, 0), memory_space=pltpu.SMEM)


def _peer_u(rows_tm, shifts_tm, gates, shifts, u3, tbl, tm):
    n_tiles = gates.shape[0]
    vm = pl.BlockSpec((1, PEER_SEL, tm), lambda i: (i, 0, 0))
    return pl.pallas_call(
        _peer_u_kernel,
        grid=(n_tiles,),
        in_specs=[_smem_tile(tm), _smem_tile(tm), pl.BlockSpec((tm, SUBLANES, LANES), lambda i: (i, 0, 0)), vm, vm,
                  _resident(tbl.shape)],
        out_specs=pl.BlockSpec((tm, 2 * PEER_SEL), lambda i: (i, 0)),
        out_shape=jax.ShapeDtypeStruct((n_tiles * tm, 2 * PEER_SEL), F32),
        scratch_shapes=[pltpu.VMEM((tm * PEER_SEL, LANES), F32)],
        compiler_params=_params(("arbitrary",)),
        name="peer_u",
    )(rows_tm, shifts_tm, u3, gates, shifts, tbl)


def _peer_v(rows, coef_pairs, h3, g_final, tbl, tm):
    n_tiles = h3.shape[0] // tm
    tok = pl.BlockSpec((tm, SUBLANES, LANES), lambda i: (i, 0, 0))
    expand, diag = _v_expand_matrices()
    return pl.pallas_call(
        _peer_v_kernel,
        grid=(n_tiles,),
        in_specs=[_smem_tile(tm), pl.BlockSpec((tm, 2 * PEER_SEL), lambda i: (i, 0)), tok,
                  _resident(g_final.shape), _resident(expand.shape), _resident(diag.shape), _resident(tbl.shape)],
        out_specs=tok,
        out_shape=jax.ShapeDtypeStruct(h3.shape, F32),
        scratch_shapes=[pltpu.VMEM((2, tm, V_COLS), F32)],
        compiler_params=_params(("arbitrary",)),
        name="peer_v",
    )(rows, coef_pairs, h3, g_final, expand, diag, tbl)


PROJ_TILE = 512
MERGE_TILE = 256
PEER_TILE = 128


def _trunk(x, p):
    bn, s_len, d = x.shape
    n_tok = bn * s_len
    x2 = x.reshape(n_tok, d)
    qa, ka, va, qb, kb, vb, ga, gb = _inproj(x2, p["norm_mix"], p["w_in"], PROJ_TILE)
    seq = lambda t: t.reshape(bn, s_len, t.shape[-1])
    dil_outs = []
    for group, (window, dilation) in enumerate(DIL_GROUPS):
        assert window // (2 * dilation) == RAD
        sub_len = s_len // dilation
        assert s_len % dilation == 0 and sub_len % RAD == 0
        tq = min(256, sub_len)
        assert sub_len % tq == 0
        bias = _dilated_bias(p["t5_table"], group, dilation, tq)
        o, l = _dilated_group(seq(qa), seq(ka), seq(va), bias, group, dilation, tq)
        dil_outs.append((o.reshape(n_tok, A_OUT_WIDTH), l.reshape(n_tok, A_OUT_WIDTH)))
    assert s_len % NA_TILE == 0 and s_len // GRID_W >= NB_ROWS
    ob = _neighbourhood(seq(qb), seq(kb), seq(vb), p["na_bias"]).reshape(n_tok, B_WIDTH)
    h, u, qp = _merge(x2, dil_outs, ob, ga, gb, p["w_proj_a"], p["w_proj_b"], p["w_out"], p["norm_ffn"],
                      p["w_query"], MERGE_TILE)
    rows_tm, shifts_tm, shifts, gates = _topk(qp, p["sub_keys"], PEER_TILE)
    tile3 = lambda t: t.reshape(n_tok, SUBLANES, LANES)
    coef_pairs = _peer_u(rows_tm, shifts_tm, gates, shifts, tile3(u), p["expert_u"], PEER_TILE)
    y = _peer_v(rows_tm, coef_pairs, tile3(h), p["norm_final"], p["expert_v"], PEER_TILE)
    return y.reshape(bn, s_len, d)


def kernel(x_prompt, x_sample, norm_mix, w_in, w_proj_a, w_proj_b, w_out, rpb, norm_ffn, w_query, sub_keys,
           expert_u, expert_v, t5_table, norm_final):
    assert norm_mix.shape[0] == 1, "single-layer model"
    p = {
        "norm_mix": norm_mix[0].reshape(1, D_MODEL).astype(F32),
        "w_in": w_in[0].astype(BF16),
        "w_proj_a": w_proj_a[0].astype(BF16),
        "w_proj_b": w_proj_b[0].astype(BF16),
        "w_out": w_out[0].astype(BF16),
        "na_bias": _na_bias(rpb[0]),
        "norm_ffn": norm_ffn[0].reshape(1, D_MODEL).astype(F32),
        "w_query": w_query[0].astype(BF16),
        "sub_keys": sub_keys[0].astype(BF16),
        "expert_u": _pack_experts(expert_u[0]),
        "expert_v": _pack_experts(expert_v[0]),
        "t5_table": t5_table,
        "norm_final": norm_final.reshape(SUBLANES, LANES).astype(F32),
    }
    return (_trunk(x_prompt, p), _trunk(x_sample, p))
```

```python
import functools
import math

import numpy as np
import jax
import jax.numpy as jnp
from jax import lax
from jax.experimental import pallas as pl
from jax.experimental.pallas import tpu as pltpu

D_MODEL = 1024
HEAD_DIM = 64
DIL_GROUPS = ((128, 1), (512, 4), (2048, 16))
A_HEADS_PER_GROUP = 4
A_HEADS = A_HEADS_PER_GROUP * len(DIL_GROUPS)
A_WIDTH = A_HEADS * HEAD_DIM
A_OUT_WIDTH = A_HEADS_PER_GROUP * HEAD_DIM
T5_BUCKETS = 32
T5_MAX_DIST = 1024
GRID_W = 64
B_HEADS = 8
B_WIDTH = B_HEADS * HEAD_DIM
NB_ROWS = 8
NB_COLS = 16
PEER_HEADS = 8
PEER_NKEYS = 128
PEER_EXPERTS = PEER_NKEYS * PEER_NKEYS
PEER_QDIM = 256
PEER_HALF = PEER_QDIM // 2
PEER_TOPK = 16
PEER_SEL = PEER_HEADS * PEER_TOPK
NORM_EPS = 1e-6
NEG_INF = -1e30

RAD = 64
SUBLANES = 8
LANES = 128
HALF_EXPERTS = PEER_EXPERTS // 2
VMEM_LIMIT = 48 * 1024 * 1024

F32 = jnp.float32
BF16 = jnp.bfloat16


def _resident(shape):
    nd = len(shape)
    return pl.BlockSpec(shape, lambda *_: (0,) * nd, pipeline_mode=pl.Buffered(1))


def _params(sem):
    return pltpu.CompilerParams(dimension_semantics=sem, vmem_limit_bytes=VMEM_LIMIT)


IN_SPLITS = (A_WIDTH, A_WIDTH, A_WIDTH, B_WIDTH, B_WIDTH, B_WIDTH, D_MODEL, D_MODEL)
IN_SCALES = (HEAD_DIM ** -0.5, 1.0, 1.0, HEAD_DIM ** -0.5, 1.0, 1.0, 1.0, 1.0)


def _rms(x, g):
    return x * lax.rsqrt(jnp.mean(x * x, axis=-1, keepdims=True) + NORM_EPS) * g


def _inproj_kernel(x_ref, g_ref, w_ref, *out_refs):
    u = _rms(x_ref[...], g_ref[...]).astype(BF16)
    off = 0
    for o_ref, width, scale in zip(out_refs, IN_SPLITS, IN_SCALES):
        z = jnp.dot(u, w_ref[:, off:off + width], preferred_element_type=F32)
        if scale != 1.0:
            z = z * scale
        o_ref[...] = z.astype(o_ref.dtype)
        off += width


def _inproj(x2, g, w_bf16, tm):
    n_tok = x2.shape[0]
    return pl.pallas_call(
        _inproj_kernel,
        grid=(n_tok // tm,),
        in_specs=[pl.BlockSpec((tm, D_MODEL), lambda i: (i, 0)),
                  _resident((1, D_MODEL)),
                  _resident(w_bf16.shape)],
        out_specs=[pl.BlockSpec((tm, w), lambda i: (i, 0)) for w in IN_SPLITS],
        out_shape=[jax.ShapeDtypeStruct((n_tok, w), BF16) for w in IN_SPLITS],
        compiler_params=_params(("parallel",)),
        name="inproj",
    )(x2, g, w_bf16)


def _t5_bucket(rel):
    n = -rel
    nb = T5_BUCKETS // 2
    ret = (n < 0).astype(np.int32) * nb
    n = np.abs(n)
    max_exact = nb // 2
    large = max_exact + (np.log(np.maximum(n, 1) / max_exact) / math.log(T5_MAX_DIST / max_exact)
                         * (nb - max_exact)).astype(np.int32)
    large = np.minimum(large, nb - 1)
    return (ret + np.where(n < max_exact, n, large)).astype(np.int32)


def _dilated_bias(t5_table, group, dilation, tq):
    rel = np.arange(tq + 2 * RAD)[None, :] - RAD - np.arange(tq)[:, None]
    tab = t5_table[:, group * A_HEADS_PER_GROUP:(group + 1) * A_HEADS_PER_GROUP].astype(F32)
    in_band = np.abs(rel) <= RAD
    bucket = np.where(in_band, _t5_bucket(rel * dilation), -1)
    bias = jnp.full((A_HEADS_PER_GROUP,) + rel.shape, NEG_INF, F32)
    for b in np.unique(bucket[in_band]):
        bias = jnp.where(jnp.asarray(bucket == b)[None], tab[int(b)][:, None, None], bias)
    return bias


def _dilated_kernel(q_ref, kp_ref, kc_ref, kn_ref, vp_ref, vc_ref, vn_ref, bias_ref, o_ref, l_ref, *, tq, n_tiles):
    i = pl.program_id(2)
    q = q_ref[0]
    k_all = jnp.concatenate([kp_ref[0], kc_ref[0], kn_ref[0]], axis=0)
    v_all = jnp.concatenate([vp_ref[0], vc_ref[0], vn_ref[0]], axis=0)
    nk = tq + 2 * RAD
    col = lax.broadcasted_iota(jnp.int32, (tq, nk), 1)
    first_valid = jnp.where(i == 0, RAD, 0)
    end_valid = jnp.where(i == n_tiles - 1, tq + RAD, nk)
    valid = (col >= first_valid) & (col < end_valid)
    lane = lax.broadcasted_iota(jnp.int32, (tq, A_OUT_WIDTH), 1)
    o_acc = jnp.zeros((tq, A_OUT_WIDTH), F32)
    l_acc = jnp.zeros((tq, A_OUT_WIDTH), F32)
    for h in range(A_HEADS_PER_GROUP):
        head = (lane >= h * HEAD_DIM) & (lane < (h + 1) * HEAD_DIM)
        qh = jnp.where(head, q, jnp.zeros_like(q))
        s = lax.dot_general(qh, k_all, (((1,), (1,)), ((), ())), preferred_element_type=F32)
        s = jnp.where(valid, s + bias_ref[h], NEG_INF)
        m = jnp.max(s, axis=-1, keepdims=True)
        p = jnp.exp(s - m)
        den = jnp.sum(p, axis=-1, keepdims=True)
        o = jnp.dot(p.astype(BF16), v_all, preferred_element_type=F32) / den
        lse = m + jnp.log(den)
        o_acc = jnp.where(head, o, o_acc)
        l_acc = jnp.where(head, lse, l_acc)
    o_ref[0] = o_acc
    l_ref[0] = l_acc


def _dilated_group(qa, ka, va, bias, group, dilation, tq):
    bn, s_len, _ = qa.shape
    sub_len = s_len // dilation
    n_tiles = sub_len // tq
    blk = tq // RAD
    n_rad = sub_len // RAD
    ngrp = len(DIL_GROUPS)
    view = lambda t: t.reshape(bn, sub_len, dilation * A_WIDTH)
    cur = pl.BlockSpec((1, tq, A_OUT_WIDTH), lambda b, r, i: (b, i, r * ngrp + group))
    prev = pl.BlockSpec((1, RAD, A_OUT_WIDTH), lambda b, r, i: (b, jnp.maximum(i * blk - 1, 0), r * ngrp + group))
    nxt = pl.BlockSpec((1, RAD, A_OUT_WIDTH), lambda b, r, i: (b, jnp.minimum((i + 1) * blk, n_rad - 1), r * ngrp + group))
    out = pl.BlockSpec((1, tq, A_OUT_WIDTH), lambda b, r, i: (b, i, r))
    o, l = pl.pallas_call(
        functools.partial(_dilated_kernel, tq=tq, n_tiles=n_tiles),
        grid=(bn, dilation, n_tiles),
        in_specs=[cur, prev, cur, nxt, prev, cur, nxt, _resident(bias.shape)],
        out_specs=[out, out],
        out_shape=[jax.ShapeDtypeStruct((bn, sub_len, dilation * A_OUT_WIDTH), F32)] * 2,
        compiler_params=_params(("parallel", "parallel", "parallel")),
        name=f"dilated{group}",
    )(view(qa), view(ka), view(ka), view(ka), view(va), view(va), view(va), bias)
    return o.reshape(bn, s_len, A_OUT_WIDTH), l.reshape(bn, s_len, A_OUT_WIDTH)


NA_TILE_ROWS = 8
NA_TILE = NA_TILE_ROWS * GRID_W
NA_KEYS = NB_ROWS * GRID_W


def _na_bias(rpb):
    cols = np.arange(GRID_W)
    col_start = np.clip(cols - NB_COLS // 2, 0, GRID_W - NB_COLS)
    col_valid = (cols[None, :] >= col_start[:, None]) & (cols[None, :] < col_start[:, None] + NB_COLS)
    dc = np.clip(cols[None, :] - cols[:, None] + NB_COLS - 1, 0, 2 * NB_COLS - 2)
    rpb = rpb.astype(F32)
    variants = []
    for v in range(NB_ROWS):
        dr = np.clip(np.arange(NB_ROWS) + v, 0, 2 * NB_ROWS - 2)
        tab = rpb[:, dr][:, :, dc]
        tab = jnp.where(jnp.asarray(col_valid)[None, None], tab, NEG_INF)
        variants.append(jnp.transpose(tab, (0, 2, 1, 3)).reshape(B_HEADS, GRID_W, NA_KEYS))
    return jnp.stack(variants, axis=0)


def _na_kernel(q_ref, kp_ref, kc_ref, kn_ref, vp_ref, vc_ref, vn_ref, bias_ref, o_ref, kwin, vwin, *, rows):
    i = pl.program_id(1)
    kwin[0:NA_TILE] = kp_ref[0]
    kwin[NA_TILE:2 * NA_TILE] = kc_ref[0]
    kwin[2 * NA_TILE:3 * NA_TILE] = kn_ref[0]
    vwin[0:NA_TILE] = vp_ref[0]
    vwin[NA_TILE:2 * NA_TILE] = vc_ref[0]
    vwin[2 * NA_TILE:3 * NA_TILE] = vn_ref[0]
    lane = lax.broadcasted_iota(jnp.int32, (GRID_W, 2 * HEAD_DIM), 1)
    low = lane < HEAD_DIM
    for a in range(NA_TILE_ROWS):
        r = i * NA_TILE_ROWS + a
        rs = jnp.clip(r - NB_ROWS // 2, 0, rows - NB_ROWS)
        start = pl.multiple_of((rs - (i - 1) * NA_TILE_ROWS) * GRID_W, GRID_W)
        variant = rs - r + NB_ROWS - 1
        for hp in range(B_HEADS // 2):
            cs = slice(hp * 2 * HEAD_DIM, (hp + 1) * 2 * HEAD_DIM)
            q2 = q_ref[0, a * GRID_W:(a + 1) * GRID_W, cs]
            k2 = kwin[pl.ds(start, NA_KEYS), cs]
            v2 = vwin[pl.ds(start, NA_KEYS), cs]
            o2 = None
            for hh in range(2):
                mine = low if hh == 0 else jnp.logical_not(low)
                qh = jnp.where(mine, q2, jnp.zeros_like(q2))
                s = lax.dot_general(qh, k2, (((1,), (1,)), ((), ())), preferred_element_type=F32)
                s = s + bias_ref[variant, hp * 2 + hh]
                m = jnp.max(s, axis=-1, keepdims=True)
                p = jnp.exp(s - m)
                den = jnp.sum(p, axis=-1, keepdims=True)
                o = jnp.dot(p.astype(BF16), v2, preferred_element_type=F32) / den
                o2 = o if hh == 0 else jnp.where(low, o2, o)
            o_ref[0, a * GRID_W:(a + 1) * GRID_W, cs] = o2.astype(o_ref.dtype)


def _neighbourhood(qb, kb, vb, bias):
    bn, s_len, _ = qb.shape
    rows = s_len // GRID_W
    n_tiles = rows // NA_TILE_ROWS
    cur = pl.BlockSpec((1, NA_TILE, B_WIDTH), lambda b, i: (b, i, 0))
    prev = pl.BlockSpec((1, NA_TILE, B_WIDTH), lambda b, i: (b, jnp.maximum(i - 1, 0), 0))
    nxt = pl.BlockSpec((1, NA_TILE, B_WIDTH), lambda b, i: (b, jnp.minimum(i + 1, n_tiles - 1), 0))
    return pl.pallas_call(
        functools.partial(_na_kernel, rows=rows),
        grid=(bn, n_tiles),
        in_specs=[cur, prev, cur, nxt, prev, cur, nxt, _resident(bias.shape)],
        out_specs=cur,
        out_shape=jax.ShapeDtypeStruct((bn, s_len, B_WIDTH), BF16),
        scratch_shapes=[pltpu.VMEM((3 * NA_TILE, B_WIDTH), BF16)] * 2,
        compiler_params=_params(("parallel", "parallel")),
        name="neighbourhood",
    )(qb, kb, kb, kb, vb, vb, vb, bias)


def _merge_kernel(x_ref, o1_ref, l1_ref, o2_ref, l2_ref, o3_ref, l3_ref, ob_ref, ga_ref, gb_ref,
                  wpa_ref, wpb_ref, wout_ref, gffn_ref, wq_ref, h_ref, u_ref, qp_ref):
    l1, l2, l3 = l1_ref[...], l2_ref[...], l3_ref[...]
    m = jnp.maximum(jnp.maximum(l1, l2), l3)
    w1, w2, w3 = jnp.exp(l1 - m), jnp.exp(l2 - m), jnp.exp(l3 - m)
    oa = (w1 * o1_ref[...] + w2 * o2_ref[...] + w3 * o3_ref[...]) / (w1 + w2 + w3)
    pa = jnp.dot(oa.astype(BF16), wpa_ref[...], preferred_element_type=F32)
    pb = jnp.dot(ob_ref[...], wpb_ref[...], preferred_element_type=F32)
    merged = jax.nn.sigmoid(ga_ref[...].astype(F32)) * pa + jax.nn.sigmoid(gb_ref[...].astype(F32)) * pb
    h = x_ref[...] + jnp.dot(merged.astype(BF16), wout_ref[...], preferred_element_type=F32)
    u = _rms(h, gffn_ref[...])
    h_ref[...] = h
    u_ref[...] = u
    qp_ref[...] = jnp.dot(u.astype(BF16), wq_ref[...], preferred_element_type=F32).astype(qp_ref.dtype)


def _merge(x2, dil_outs, ob, ga, gb, wpa, wpb, wout, gffn, wq, tm):
    n_tok = x2.shape[0]
    tok = lambda w: pl.BlockSpec((tm, w), lambda i: (i, 0))
    ins = [x2]
    specs = [tok(D_MODEL)]
    for o, l in dil_outs:
        ins += [o, l]
        specs += [tok(A_OUT_WIDTH), tok(A_OUT_WIDTH)]
    ins += [ob, ga, gb, wpa, wpb, wout, gffn, wq]
    specs += [tok(B_WIDTH), tok(D_MODEL), tok(D_MODEL), _resident(wpa.shape), _resident(wpb.shape),
              _resident(wout.shape), _resident(gffn.shape), _resident(wq.shape)]
    qw = PEER_HEADS * PEER_QDIM
    return pl.pallas_call(
        _merge_kernel,
        grid=(n_tok // tm,),
        in_specs=specs,
        out_specs=[tok(D_MODEL), tok(D_MODEL), tok(qw)],
        out_shape=[jax.ShapeDtypeStruct((n_tok, D_MODEL), F32), jax.ShapeDtypeStruct((n_tok, D_MODEL), F32),
                   jax.ShapeDtypeStruct((n_tok, qw), BF16)],
        compiler_params=_params(("parallel",)),
        name="merge",
    )(*ins)


def _take_top(s, count, val_ref, pick_ref, payload=None):
    n = s.shape[0]
    iota = lax.broadcasted_iota(jnp.int32, s.shape, 0)
    for j in range(count):
        m = jnp.max(s, axis=0, keepdims=True)
        pos = jnp.min(jnp.where(s == m, iota, n), axis=0, keepdims=True)
        hit = iota == pos
        val_ref[j:j + 1, :] = m
        pick_ref[j:j + 1, :] = pos if payload is None else jnp.max(jnp.where(hit, payload, -1), axis=0, keepdims=True)
        s = jnp.where(hit, -jnp.inf, s)


PAIR_COUNTS = tuple(PEER_TOPK // (a + 1) for a in range(PEER_TOPK))
N_PAIRS = sum(PAIR_COUNTS)
PAIR_ROWS = -(-N_PAIRS // SUBLANES) * SUBLANES


def _topk_kernel(qp_ref, sk_ref, row_tm_ref, sh_tm_ref, sh_ref, gate_ref,
                 row_scr, sh_scr, va_ref, ia_ref, vb_ref, ib_ref, cand_ref, cidx_ref, best_ref, pick_ref):
    tm = qp_ref.shape[0]
    head = pl.program_id(1)
    for p, (val_ref, idx_ref) in enumerate(((va_ref, ia_ref), (vb_ref, ib_ref))):
        q = qp_ref[:, p * PEER_HALF:(p + 1) * PEER_HALF]
        s = lax.dot_general(sk_ref[p], q, (((1,), (1,)), ((), ())), preferred_element_type=F32)
        _take_top(s, PEER_TOPK, val_ref, idx_ref)
    cand_ref[N_PAIRS:, :] = jnp.full((PAIR_ROWS - N_PAIRS, tm), -jnp.inf, F32)
    cidx_ref[N_PAIRS:, :] = jnp.zeros((PAIR_ROWS - N_PAIRS, tm), jnp.int32)
    at = 0
    for a, count in enumerate(PAIR_COUNTS):
        cand_ref[at:at + count, :] = va_ref[a:a + 1, :] + vb_ref[0:count, :]
        cidx_ref[at:at + count, :] = ia_ref[a:a + 1, :] * PEER_NKEYS + ib_ref[0:count, :]
        at += count
    _take_top(cand_ref[...], PEER_TOPK, best_ref, pick_ref, payload=cidx_ref[...])
    best = best_ref[...]
    e = jnp.exp(best - best[0:1])
    gate_ref[0] = e / jnp.sum(e, axis=0, keepdims=True)
    experts = pick_ref[...]
    shift = jnp.where(experts < HALF_EXPERTS, 16, 0)
    sh_ref[0] = shift
    slots = pl.ds(pl.multiple_of(head * PEER_TOPK, PEER_TOPK), PEER_TOPK)
    row_scr[slots, :] = (experts & (HALF_EXPERTS - 1)) * SUBLANES
    sh_scr[slots, :] = shift

    @pl.when(head == PEER_HEADS - 1)
    def _():
        row_tm_ref[0] = row_scr[...].T
        sh_tm_ref[0] = sh_scr[...].T


def _topk(qp, sk_bf16, tm):
    n_tok = qp.shape[0]
    n_tiles = n_tok // tm
    per_head = pl.BlockSpec((1, PEER_TOPK, tm), lambda i, h: (i, h, 0))
    per_tile = pl.BlockSpec((1, tm, PEER_SEL), lambda i, h: (i, 0, 0))
    slot_major = (n_tiles, PEER_SEL, tm)
    token_major = (n_tiles, tm, PEER_SEL)
    return pl.pallas_call(
        _topk_kernel,
        grid=(n_tiles, PEER_HEADS),
        in_specs=[pl.BlockSpec((tm, PEER_QDIM), lambda i, h: (i, h)), _resident(sk_bf16.shape)],
        out_specs=[per_tile, per_tile, per_head, per_head],
        out_shape=[jax.ShapeDtypeStruct(token_major, jnp.int32), jax.ShapeDtypeStruct(token_major, jnp.int32),
                   jax.ShapeDtypeStruct(slot_major, jnp.int32), jax.ShapeDtypeStruct(slot_major, F32)],
        scratch_shapes=[pltpu.VMEM((PEER_SEL, tm), jnp.int32)] * 2
        + [pltpu.VMEM((PEER_TOPK, tm), F32), pltpu.VMEM((PEER_TOPK, tm), jnp.int32)] * 2
        + [pltpu.VMEM((PAIR_ROWS, tm), F32), pltpu.VMEM((PAIR_ROWS, tm), jnp.int32)]
        + [pltpu.VMEM((PEER_TOPK, tm), F32), pltpu.VMEM((PEER_TOPK, tm), jnp.int32)],
        compiler_params=_params(("parallel", "arbitrary")),
        name="peer_topk",
    )(qp, sk_bf16)


HIGH_HALF = -65536


def _pack_experts(tbl):
    b = lax.bitcast_convert_type(tbl.astype(BF16), jnp.uint16).astype(jnp.uint32)
    w = (b[HALF_EXPERTS:] << 16) | b[:HALF_EXPERTS]
    return lax.bitcast_convert_type(w, jnp.int32).reshape(HALF_EXPERTS * SUBLANES, LANES)


def _table_tile(tbl_ref, offset):
    return tbl_ref[pl.ds(pl.multiple_of(offset, SUBLANES), SUBLANES), :]


def _expert_row(tbl_ref, offset, shift):
    return lax.bitcast_convert_type((_table_tile(tbl_ref, offset) << shift) & HIGH_HALF, F32)


def _sublane_sums(parts, masks):
    shift = SUBLANES // 2
    for mask in masks:
        half = len(parts) // 2
        nxt = []
        for j in range(half):
            a, b = parts[j], parts[j + half]
            nxt.append(jnp.where(mask, a, pltpu.roll(b, shift, 0)) + jnp.where(mask, pltpu.roll(a, SUBLANES - shift, 0), b))
        parts = nxt
        shift //= 2
    return parts[0]


PEER_GROUP = SUBLANES
GROUPS_PER_TOKEN = PEER_SEL // PEER_GROUP


def _peer_u_kernel(row_sm, sh_sm, x_ref, gate_ref, sh_ref, tbl_ref, pair_ref, sums_ref):
    tm = x_ref.shape[0]
    sub = lax.broadcasted_iota(jnp.int32, (SUBLANES, LANES), 0)
    masks = (sub < 4, (sub & 3) < 2, (sub & 1) == 0)
    lane = lax.broadcasted_iota(jnp.int32, (PEER_SEL, tm), 1)

    def token(t, carry):
        x = x_ref[t]
        for g in range(GROUPS_PER_TOKEN):
            prods = [_expert_row(tbl_ref, row_sm[0, t, g * PEER_GROUP + j], sh_sm[0, t, g * PEER_GROUP + j]) * x
                     for j in range(PEER_GROUP)]
            at = pl.multiple_of(t * PEER_SEL + g * SUBLANES, SUBLANES)
            sums_ref[pl.ds(at, SUBLANES), :] = _sublane_sums(prods, masks)
        return carry

    lax.fori_loop(0, tm, token, 0)

    def finish(i, act):
        for j in range(SUBLANES):
            t = i * SUBLANES + j
            rows = sums_ref[pl.ds(pl.multiple_of(t * PEER_SEL, PEER_SEL), PEER_SEL), :]
            dots = jnp.sum(rows, axis=-1, keepdims=True)
            act = jnp.where(lane == t, dots, act)
        return act

    act = lax.fori_loop(0, tm // SUBLANES, finish, jnp.zeros((PEER_SEL, tm), F32))
    coef = gate_ref[0] * jax.nn.gelu(act)
    low = sh_ref[0] == 16
    pair_ref[:, 0:PEER_SEL] = jnp.where(low, coef, 0.0).T
    pair_ref[:, PEER_SEL:] = jnp.where(low, 0.0, coef).T


PACKED_ROWS = 2 * SUBLANES
V_CHUNK = 32
V_CHUNK_COLS = V_CHUNK * PACKED_ROWS
V_COLS = PEER_SEL * PACKED_ROWS
V_TOKENS_PER_STEP = 4


def _v_expand_matrices():
    col = np.arange(V_COLS)
    src = np.arange(2 * PEER_SEL)
    e = (src[:, None] % PEER_SEL == col[None, :] // PACKED_ROWS) & (src[:, None] // PEER_SEL == col[None, :] % 2)
    d = (col[None, :] % PACKED_ROWS) // 2 == np.arange(SUBLANES)[:, None]
    return jnp.asarray(e, BF16), jnp.asarray(d, F32)


def _peer_v_kernel(row_sm, c2_ref, h_ref, g_ref, e_ref, d_ref, tbl_ref, y_ref, ce_ref):
    tm = h_ref.shape[0]
    c2 = c2_ref[...]
    c_hi = c2.astype(BF16)
    c_lo = (c2 - c_hi.astype(F32)).astype(BF16)
    ce_ref[0] = jnp.dot(c_hi, e_ref[...], preferred_element_type=F32)
    ce_ref[1] = jnp.dot(c_lo, e_ref[...], preferred_element_type=F32)
    diag = d_ref[...]
    half_cols = V_CHUNK_COLS // 2

    def token(t):
        lhs = jnp.concatenate([ce_ref[0, pl.ds(t, 1), :] * diag, ce_ref[1, pl.ds(t, 1), :] * diag],
                              axis=0).astype(BF16)
        acc = None
        for c in range(PEER_SEL // V_CHUNK):
            tiles = [pltpu.bitcast(_table_tile(tbl_ref, row_sm[0, t, c * V_CHUNK + j]), BF16) for j in range(V_CHUNK)]
            w = jnp.concatenate([jnp.concatenate(tiles[:V_CHUNK // 2], axis=0),
                                 jnp.concatenate(tiles[V_CHUNK // 2:], axis=0)], axis=1)
            c0 = c * V_CHUNK_COLS
            left = jnp.concatenate([lhs[:, c0:c0 + half_cols], lhs[:, c0 + half_cols:c0 + V_CHUNK_COLS]], axis=0)
            out = jnp.dot(left, w, preferred_element_type=F32)
            acc = out if acc is None else acc + out
        first = acc[0:SUBLANES, 0:LANES] + acc[SUBLANES:2 * SUBLANES, 0:LANES]
        second = acc[2 * SUBLANES:3 * SUBLANES, LANES:] + acc[3 * SUBLANES:, LANES:]
        y_ref[t] = h_ref[t] + (first + second)

    def tokens(i, carry):
        for j in range(V_TOKENS_PER_STEP):
            token(i * V_TOKENS_PER_STEP + j)
        return carry

    lax.fori_loop(0, tm // V_TOKENS_PER_STEP, tokens, 0)
    h = y_ref[...]
    ms = jnp.sum(jnp.sum(h * h, axis=2, keepdims=True), axis=1, keepdims=True) * (1.0 / D_MODEL)
    y_ref[...] = h * lax.rsqrt(ms + NORM_EPS) * g_ref[...]


def _smem_tile(tm):
    return pl.BlockSpec((1, tm, PEER_SEL), lambda i: (i, 0, 0), memory_space=pltpu.SMEM)


def _peer_u(rows_tm, shifts_tm, gates, shifts, u3, tbl, tm):
    n_tiles = gates.shape[0]
    vm = pl.BlockSpec((1, PEER_SEL, tm), lambda i: (i, 0, 0))
    return pl.pallas_call(
        _peer_u_kernel,
        grid=(n_tiles,),
        in_specs=[_smem_tile(tm), _smem_tile(tm), pl.BlockSpec((tm, SUBLANES, LANES), lambda i: (i, 0, 0)), vm, vm,
                  _resident(tbl.shape)],
        out_specs=pl.BlockSpec((tm, 2 * PEER_SEL), lambda i: (i, 0)),
        out_shape=jax.ShapeDtypeStruct((n_tiles * tm, 2 * PEER_SEL), F32),
        scratch_shapes=[pltpu.VMEM((tm * PEER_SEL, LANES), F32)],
        compiler_params=_params(("arbitrary",)),
        name="peer_u",
    )(rows_tm, shifts_tm, u3, gates, shifts, tbl)


def _peer_v(rows, coef_pairs, h3, g_final, tbl, tm):
    n_tiles = h3.shape[0] // tm
    tok = pl.BlockSpec((tm, SUBLANES, LANES), lambda i: (i, 0, 0))
    expand, diag = _v_expand_matrices()
    return pl.pallas_call(
        _peer_v_kernel,
        grid=(n_tiles,),
        in_specs=[_smem_tile(tm), pl.BlockSpec((tm, 2 * PEER_SEL), lambda i: (i, 0)), tok,
                  _resident(g_final.shape), _resident(expand.shape), _resident(diag.shape), _resident(tbl.shape)],
        out_specs=tok,
        out_shape=jax.ShapeDtypeStruct(h3.shape, F32),
        scratch_shapes=[pltpu.VMEM((2, tm, V_COLS), F32)],
        compiler_params=_params(("arbitrary",)),
        name="peer_v",
    )(rows, coef_pairs, h3, g_final, expand, diag, tbl)


PROJ_TILE = 512
MERGE_TILE = 256
PEER_TILE = 128


def _trunk(x, p):
    bn, s_len, d = x.shape
    n_tok = bn * s_len
    x2 = x.reshape(n_tok, d)
    qa, ka, va, qb, kb, vb, ga, gb = _inproj(x2, p["norm_mix"], p["w_in"], PROJ_TILE)
    seq = lambda t: t.reshape(bn, s_len, t.shape[-1])
    dil_outs = []
    for group, (window, dilation) in enumerate(DIL_GROUPS):
        assert window // (2 * dilation) == RAD
        sub_len = s_len // dilation
        assert s_len % dilation == 0 and sub_len % RAD == 0
        tq = min(256, sub_len)
        assert sub_len % tq == 0
        bias = _dilated_bias(p["t5_table"], group, dilation, tq)
        o, l = _dilated_group(seq(qa), seq(ka), seq(va), bias, group, dilation, tq)
        dil_outs.append((o.reshape(n_tok, A_OUT_WIDTH), l.reshape(n_tok, A_OUT_WIDTH)))
    assert s_len % NA_TILE == 0 and s_len // GRID_W >= NB_ROWS
    ob = _neighbourhood(seq(qb), seq(kb), seq(vb), p["na_bias"]).reshape(n_tok, B_WIDTH)
    h, u, qp = _merge(x2, dil_outs, ob, ga, gb, p["w_proj_a"], p["w_proj_b"], p["w_out"], p["norm_ffn"],
                      p["w_query"], MERGE_TILE)
    rows_tm, shifts_tm, shifts, gates = _topk(qp, p["sub_keys"], PEER_TILE)
    tile3 = lambda t: t.reshape(n_tok, SUBLANES, LANES)
    coef_pairs = _peer_u(rows_tm, shifts_tm, gates, shifts, tile3(u), p["expert_u"], PEER_TILE)
    y = _peer_v(rows_tm, coef_pairs, tile3(h), p["norm_final"], p["expert_v"], PEER_TILE)
    return y.reshape(bn, s_len, d)


def kernel(x_prompt, x_sample, norm_mix, w_in, w_proj_a, w_proj_b, w_out, rpb, norm_ffn, w_query, sub_keys,
           expert_u, expert_v, t5_table, norm_final):
    assert norm_mix.shape[0] == 1, "single-layer model"
    p = {
        "norm_mix": norm_mix[0].reshape(1, D_MODEL).astype(F32),
        "w_in": w_in[0].astype(BF16),
        "w_proj_a": w_proj_a[0].astype(BF16),
        "w_proj_b": w_proj_b[0].astype(BF16),
        "w_out": w_out[0].astype(BF16),
        "na_bias": _na_bias(rpb[0]),
        "norm_ffn": norm_ffn[0].reshape(1, D_MODEL).astype(F32),
        "w_query": w_query[0].astype(BF16),
        "sub_keys": sub_keys[0].astype(BF16),
        "expert_u": _pack_experts(expert_u[0]),
        "expert_v": _pack_experts(expert_v[0]),
        "t5_table": t5_table,
        "norm_final": norm_final.reshape(SUBLANES, LANES).astype(F32),
    }
    return (_trunk(x_prompt, p), _trunk(x_sample, p))
```

```python
import functools
import math

import numpy as np
import jax
import jax.numpy as jnp
from jax import lax
from jax.experimental import pallas as pl
from jax.experimental.pallas import tpu as pltpu

D_MODEL = 1024
HEAD_DIM = 64
DIL_GROUPS = ((128, 1), (512, 4), (2048, 16))
A_HEADS_PER_GROUP = 4
A_HEADS = A_HEADS_PER_GROUP * len(DIL_GROUPS)
A_WIDTH = A_HEADS * HEAD_DIM
A_OUT_WIDTH = A_HEADS_PER_GROUP * HEAD_DIM
T5_BUCKETS = 32
T5_MAX_DIST = 1024
GRID_W = 64
B_HEADS = 8
B_WIDTH = B_HEADS * HEAD_DIM
NB_ROWS = 8
NB_COLS = 16
PEER_HEADS = 8
PEER_NKEYS = 128
PEER_EXPERTS = PEER_NKEYS * PEER_NKEYS
PEER_QDIM = 256
PEER_HALF = PEER_QDIM // 2
PEER_TOPK = 16
PEER_SEL = PEER_HEADS * PEER_TOPK
NORM_EPS = 1e-6
NEG_INF = -1e30

RAD = 64
SUBLANES = 8
LANES = 128
HALF_EXPERTS = PEER_EXPERTS // 2
VMEM_LIMIT = 48 * 1024 * 1024

F32 = jnp.float32
BF16 = jnp.bfloat16


def _resident(shape):
    nd = len(shape)
    return pl.BlockSpec(shape, lambda *_: (0,) * nd, pipeline_mode=pl.Buffered(1))


def _params(sem):
    return pltpu.CompilerParams(dimension_semantics=sem, vmem_limit_bytes=VMEM_LIMIT)


IN_SPLITS = (A_WIDTH, A_WIDTH, A_WIDTH, B_WIDTH, B_WIDTH, B_WIDTH, D_MODEL, D_MODEL)
IN_SCALES = (HEAD_DIM ** -0.5, 1.0, 1.0, HEAD_DIM ** -0.5, 1.0, 1.0, 1.0, 1.0)


def _rms(x, g):
    return x * lax.rsqrt(jnp.mean(x * x, axis=-1, keepdims=True) + NORM_EPS) * g


def _inproj_kernel(x_ref, g_ref, w_ref, *out_refs):
    u = _rms(x_ref[...], g_ref[...]).astype(BF16)
    off = 0
    for o_ref, width, scale in zip(out_refs, IN_SPLITS, IN_SCALES):
        z = jnp.dot(u, w_ref[:, off:off + width], preferred_element_type=F32)
        if scale != 1.0:
            z = z * scale
        o_ref[...] = z.astype(o_ref.dtype)
        off += width


def _inproj(x2, g, w_bf16, tm):
    n_tok = x2.shape[0]
    return pl.pallas_call(
        _inproj_kernel,
        grid=(n_tok // tm,),
        in_specs=[pl.BlockSpec((tm, D_MODEL), lambda i: (i, 0)),
                  _resident((1, D_MODEL)),
                  _resident(w_bf16.shape)],
        out_specs=[pl.BlockSpec((tm, w), lambda i: (i, 0)) for w in IN_SPLITS],
        out_shape=[jax.ShapeDtypeStruct((n_tok, w), BF16) for w in IN_SPLITS],
        compiler_params=_params(("parallel",)),
        name="inproj",
    )(x2, g, w_bf16)


def _t5_bucket(rel):
    n = -rel
    nb = T5_BUCKETS // 2
    ret = (n < 0).astype(np.int32) * nb
    n = np.abs(n)
    max_exact = nb // 2
    large = max_exact + (np.log(np.maximum(n, 1) / max_exact) / math.log(T5_MAX_DIST / max_exact)
                         * (nb - max_exact)).astype(np.int32)
    large = np.minimum(large, nb - 1)
    return (ret + np.where(n < max_exact, n, large)).astype(np.int32)


def _dilated_bias(t5_table, group, dilation, tq):
    rel = np.arange(tq + 2 * RAD)[None, :] - RAD - np.arange(tq)[:, None]
    tab = t5_table[:, group * A_HEADS_PER_GROUP:(group + 1) * A_HEADS_PER_GROUP].astype(F32)
    in_band = np.abs(rel) <= RAD
    bucket = np.where(in_band, _t5_bucket(rel * dilation), -1)
    bias = jnp.full((A_HEADS_PER_GROUP,) + rel.shape, NEG_INF, F32)
    for b in np.unique(bucket[in_band]):
        bias = jnp.where(jnp.asarray(bucket == b)[None], tab[int(b)][:, None, None], bias)
    return bias


def _dilated_kernel(q_ref, kp_ref, kc_ref, kn_ref, vp_ref, vc_ref, vn_ref, bias_ref, o_ref, l_ref, *, tq, n_tiles):
    i = pl.program_id(2)
    q = q_ref[0]
    k_all = jnp.concatenate([kp_ref[0], kc_ref[0], kn_ref[0]], axis=0)
    v_all = jnp.concatenate([vp_ref[0], vc_ref[0], vn_ref[0]], axis=0)
    nk = tq + 2 * RAD
    col = lax.broadcasted_iota(jnp.int32, (tq, nk), 1)
    first_valid = jnp.where(i == 0, RAD, 0)
    end_valid = jnp.where(i == n_tiles - 1, tq + RAD, nk)
    valid = (col >= first_valid) & (col < end_valid)
    lane = lax.broadcasted_iota(jnp.int32, (tq, A_OUT_WIDTH), 1)
    o_acc = jnp.zeros((tq, A_OUT_WIDTH), F32)
    l_acc = jnp.zeros((tq, A_OUT_WIDTH), F32)
    for h in range(A_HEADS_PER_GROUP):
        head = (lane >= h * HEAD_DIM) & (lane < (h + 1) * HEAD_DIM)
        qh = jnp.where(head, q, jnp.zeros_like(q))
        s = lax.dot_general(qh, k_all, (((1,), (1,)), ((), ())), preferred_element_type=F32)
        s = jnp.where(valid, s + bias_ref[h], NEG_INF)
        m = jnp.max(s, axis=-1, keepdims=True)
        p = jnp.exp(s - m)
        den = jnp.sum(p, axis=-1, keepdims=True)
        o = jnp.dot(p.astype(BF16), v_all, preferred_element_type=F32) / den
        lse = m + jnp.log(den)
        o_acc = jnp.where(head, o, o_acc)
        l_acc = jnp.where(head, lse, l_acc)
    o_ref[0] = o_acc
    l_ref[0] = l_acc


def _dilated_group(qa, ka, va, bias, group, dilation, tq):
    bn, s_len, _ = qa.shape
    sub_len = s_len // dilation
    n_tiles = sub_len // tq
    blk = tq // RAD
    n_rad = sub_len // RAD
    ngrp = len(DIL_GROUPS)
    view = lambda t: t.reshape(bn, sub_len, dilation * A_WIDTH)
    cur = pl.BlockSpec((1, tq, A_OUT_WIDTH), lambda b, r, i: (b, i, r * ngrp + group))
    prev = pl.BlockSpec((1, RAD, A_OUT_WIDTH), lambda b, r, i: (b, jnp.maximum(i * blk - 1, 0), r * ngrp + group))
    nxt = pl.BlockSpec((1, RAD, A_OUT_WIDTH), lambda b, r, i: (b, jnp.minimum((i + 1) * blk, n_rad - 1), r * ngrp + group))
    out = pl.BlockSpec((1, tq, A_OUT_WIDTH), lambda b, r, i: (b, i, r))
    o, l = pl.pallas_call(
        functools.partial(_dilated_kernel, tq=tq, n_tiles=n_tiles),
        grid=(bn, dilation, n_tiles),
        in_specs=[cur, prev, cur, nxt, prev, cur, nxt, _resident(bias.shape)],
        out_specs=[out, out],
        out_shape=[jax.ShapeDtypeStruct((bn, sub_len, dilation * A_OUT_WIDTH), F32)] * 2,
        compiler_params=_params(("parallel", "parallel", "parallel")),
        name=f"dilated{group}",
    )(view(qa), view(ka), view(ka), view(ka), view(va), view(va), view(va), bias)
    return o.reshape(bn, s_len, A_OUT_WIDTH), l.reshape(bn, s_len, A_OUT_WIDTH)


NA_TILE_ROWS = 8
NA_TILE = NA_TILE_ROWS * GRID_W
NA_KEYS = NB_ROWS * GRID_W


def _na_bias(rpb):
    cols = np.arange(GRID_W)
    col_start = np.clip(cols - NB_COLS // 2, 0, GRID_W - NB_COLS)
    col_valid = (cols[None, :] >= col_start[:, None]) & (cols[None, :] < col_start[:, None] + NB_COLS)
    dc = np.clip(cols[None, :] - cols[:, None] + NB_COLS - 1, 0, 2 * NB_COLS - 2)
    rpb = rpb.astype(F32)
    variants = []
    for v in range(NB_ROWS):
        dr = np.clip(np.arange(NB_ROWS) + v, 0, 2 * NB_ROWS - 2)
        tab = rpb[:, dr][:, :, dc]
        tab = jnp.where(jnp.asarray(col_valid)[None, None], tab, NEG_INF)
        variants.append(jnp.transpose(tab, (0, 2, 1, 3)).reshape(B_HEADS, GRID_W, NA_KEYS))
    return jnp.stack(variants, axis=0)


def _na_kernel(q_ref, kp_ref, kc_ref, kn_ref, vp_ref, vc_ref, vn_ref, bias_ref, o_ref, kwin, vwin, *, rows):
    i = pl.program_id(1)
    kwin[0:NA_TILE] = kp_ref[0]
    kwin[NA_TILE:2 * NA_TILE] = kc_ref[0]
    kwin[2 * NA_TILE:3 * NA_TILE] = kn_ref[0]
    vwin[0:NA_TILE] = vp_ref[0]
    vwin[NA_TILE:2 * NA_TILE] = vc_ref[0]
    vwin[2 * NA_TILE:3 * NA_TILE] = vn_ref[0]
    lane = lax.broadcasted_iota(jnp.int32, (GRID_W, 2 * HEAD_DIM), 1)
    low = lane < HEAD_DIM
    for a in range(NA_TILE_ROWS):
        r = i * NA_TILE_ROWS + a
        rs = jnp.clip(r - NB_ROWS // 2, 0, rows - NB_ROWS)
        start = pl.multiple_of((rs - (i - 1) * NA_TILE_ROWS) * GRID_W, GRID_W)
        variant = rs - r + NB_ROWS - 1
        for hp in range(B_HEADS // 2):
            cs = slice(hp * 2 * HEAD_DIM, (hp + 1) * 2 * HEAD_DIM)
            q2 = q_ref[0, a * GRID_W:(a + 1) * GRID_W, cs]
            k2 = kwin[pl.ds(start, NA_KEYS), cs]
            v2 = vwin[pl.ds(start, NA_KEYS), cs]
            o2 = None
            for hh in range(2):
                mine = low if hh == 0 else jnp.logical_not(low)
                qh = jnp.where(mine, q2, jnp.zeros_like(q2))
                s = lax.dot_general(qh, k2, (((1,), (1,)), ((), ())), preferred_element_type=F32)
                s = s + bias_ref[variant, hp * 2 + hh]
                m = jnp.max(s, axis=-1, keepdims=True)
                p = jnp.exp(s - m)
                den = jnp.sum(p, axis=-1, keepdims=True)
                o = jnp.dot(p.astype(BF16), v2, preferred_element_type=F32) / den
                o2 = o if hh == 0 else jnp.where(low, o2, o)
            o_ref[0, a * GRID_W:(a + 1) * GRID_W, cs] = o2.astype(o_ref.dtype)


def _neighbourhood(qb, kb, vb, bias):
    bn, s_len, _ = qb.shape
    rows = s_len // GRID_W
    n_tiles = rows // NA_TILE_ROWS
    cur = pl.BlockSpec((1, NA_TILE, B_WIDTH), lambda b, i: (b, i, 0))
    prev = pl.BlockSpec((1, NA_TILE, B_WIDTH), lambda b, i: (b, jnp.maximum(i - 1, 0), 0))
    nxt = pl.BlockSpec((1, NA_TILE, B_WIDTH), lambda b, i: (b, jnp.minimum(i + 1, n_tiles - 1), 0))
    return pl.pallas_call(
        functools.partial(_na_kernel, rows=rows),
        grid=(bn, n_tiles),
        in_specs=[cur, prev, cur, nxt, prev, cur, nxt, _resident(bias.shape)],
        out_specs=cur,
        out_shape=jax.ShapeDtypeStruct((bn, s_len, B_WIDTH), BF16),
        scratch_shapes=[pltpu.VMEM((3 * NA_TILE, B_WIDTH), BF16)] * 2,
        compiler_params=_params(("parallel", "parallel")),
        name="neighbourhood",
    )(qb, kb, kb, kb, vb, vb, vb, bias)


def _merge_kernel(x_ref, o1_ref, l1_ref, o2_ref, l2_ref, o3_ref, l3_ref, ob_ref, ga_ref, gb_ref,
                  wpa_ref, wpb_ref, wout_ref, gffn_ref, wq_ref, h_ref, u_ref, qp_ref):
    l1, l2, l3 = l1_ref[...], l2_ref[...], l3_ref[...]
    m = jnp.maximum(jnp.maximum(l1, l2), l3)
    w1, w2, w3 = jnp.exp(l1 - m), jnp.exp(l2 - m), jnp.exp(l3 - m)
    oa = (w1 * o1_ref[...] + w2 * o2_ref[...] + w3 * o3_ref[...]) / (w1 + w2 + w3)
    pa = jnp.dot(oa.astype(BF16), wpa_ref[...], preferred_element_type=F32)
    pb = jnp.dot(ob_ref[...], wpb_ref[...], preferred_element_type=F32)
    merged = jax.nn.sigmoid(ga_ref[...].astype(F32)) * pa + jax.nn.sigmoid(gb_ref[...].astype(F32)) * pb
    h = x_ref[...] + jnp.dot(merged.astype(BF16), wout_ref[...], preferred_element_type=F32)
    u = _rms(h, gffn_ref[...])
    h_ref[...] = h
    u_ref[...] = u
    qp_ref[...] = jnp.dot(u.astype(BF16), wq_ref[...], preferred_element_type=F32).astype(qp_ref.dtype)


def _merge(x2, dil_outs, ob, ga, gb, wpa, wpb, wout, gffn, wq, tm):
    n_tok = x2.shape[0]
    tok = lambda w: pl.BlockSpec((tm, w), lambda i: (i, 0))
    ins = [x2]
    specs = [tok(D_MODEL)]
    for o, l in dil_outs:
        ins += [o, l]
        specs += [tok(A_OUT_WIDTH), tok(A_OUT_WIDTH)]
    ins += [ob, ga, gb, wpa, wpb, wout, gffn, wq]
    specs += [tok(B_WIDTH), tok(D_MODEL), tok(D_MODEL), _resident(wpa.shape), _resident(wpb.shape),
              _resident(wout.shape), _resident(gffn.shape), _resident(wq.shape)]
    qw = PEER_HEADS * PEER_QDIM
    return pl.pallas_call(
        _merge_kernel,
        grid=(n_tok // tm,),
        in_specs=specs,
        out_specs=[tok(D_MODEL), tok(D_MODEL), tok(qw)],
        out_shape=[jax.ShapeDtypeStruct((n_tok, D_MODEL), F32), jax.ShapeDtypeStruct((n_tok, D_MODEL), F32),
                   jax.ShapeDtypeStruct((n_tok, qw), BF16)],
        compiler_params=_params(("parallel",)),
        name="merge",
    )(*ins)


def _take_top(problems, count):
    state = [s for s, _, _, _ in problems]
    for j in range(count):
        for i, (_, val_ref, pick_ref, payload) in enumerate(problems):
            s = state[i]
            iota = lax.broadcasted_iota(jnp.int32, s.shape, 0).astype(F32)
            m = jnp.max(s, axis=0, keepdims=True)
            pos = jnp.min(jnp.where(s == m, iota, float(s.shape[0])), axis=0, keepdims=True)
            hit = iota == pos
            val_ref[j:j + 1, :] = m
            pick_ref[j:j + 1, :] = (pos if payload is None
                                    else jnp.max(jnp.where(hit, payload, -1.0), axis=0, keepdims=True))
            state[i] = jnp.where(hit, -jnp.inf, s)


PAIR_COUNTS = tuple(PEER_TOPK // (a + 1) for a in range(PEER_TOPK))
N_PAIRS = sum(PAIR_COUNTS)
PAIR_ROWS = -(-N_PAIRS // SUBLANES) * SUBLANES


TOPK_HEADS_PER_STEP = 4


def _topk_kernel(qp_ref, sk_ref, row_tm_ref, sh_tm_ref, sh_ref, gate_ref,
                 row_scr, sh_scr, va_ref, ia_ref, vb_ref, ib_ref, cand_ref, cidx_ref, best_ref, pick_ref):
    tm = qp_ref.shape[0]
    step = pl.program_id(1)
    stage1 = []
    for hh in range(TOPK_HEADS_PER_STEP):
        for p, (val_ref, idx_ref) in enumerate(((va_ref, ia_ref), (vb_ref, ib_ref))):
            c0 = hh * PEER_QDIM + p * PEER_HALF
            q = qp_ref[:, c0:c0 + PEER_HALF]
            s = lax.dot_general(sk_ref[p], q, (((1,), (1,)), ((), ())), preferred_element_type=F32)
            stage1.append((s, val_ref.at[hh], idx_ref.at[hh], None))
    _take_top(stage1, PEER_TOPK)
    stage2 = []
    for hh in range(TOPK_HEADS_PER_STEP):
        cand_ref[hh, N_PAIRS:, :] = jnp.full((PAIR_ROWS - N_PAIRS, tm), -jnp.inf, F32)
        cidx_ref[hh, N_PAIRS:, :] = jnp.zeros((PAIR_ROWS - N_PAIRS, tm), F32)
        at = 0
        for a, count in enumerate(PAIR_COUNTS):
            cand_ref[hh, at:at + count, :] = va_ref[hh, a:a + 1, :] + vb_ref[hh, 0:count, :]
            cidx_ref[hh, at:at + count, :] = ia_ref[hh, a:a + 1, :] * PEER_NKEYS + ib_ref[hh, 0:count, :]
            at += count
        stage2.append((cand_ref[hh], best_ref.at[hh], pick_ref.at[hh], cidx_ref[hh]))
    _take_top(stage2, PEER_TOPK)
    rows_per_step = TOPK_HEADS_PER_STEP * PEER_TOPK
    best = best_ref[...]
    e = jnp.exp(best - best[:, 0:1])
    gate_ref[0] = (e / jnp.sum(e, axis=1, keepdims=True)).reshape(rows_per_step, tm)
    experts = pick_ref[...].astype(jnp.int32).reshape(rows_per_step, tm)
    shift = jnp.where(experts < HALF_EXPERTS, 16, 0)
    sh_ref[0] = shift
    slots = pl.ds(pl.multiple_of(step * rows_per_step, rows_per_step), rows_per_step)
    row_scr[slots, :] = (experts & (HALF_EXPERTS - 1)) * SUBLANES
    sh_scr[slots, :] = shift

    @pl.when(step == PEER_HEADS // TOPK_HEADS_PER_STEP - 1)
    def _():
        row_tm_ref[0] = row_scr[...].T
        sh_tm_ref[0] = sh_scr[...].T


def _topk(qp, sk_bf16, tm):
    n_tok = qp.shape[0]
    n_tiles = n_tok // tm
    hps = TOPK_HEADS_PER_STEP
    per_step = pl.BlockSpec((1, hps * PEER_TOPK, tm), lambda i, h: (i, h, 0))
    per_tile = pl.BlockSpec((1, tm, PEER_SEL), lambda i, h: (i, 0, 0))
    slot_major = (n_tiles, PEER_SEL, tm)
    token_major = (n_tiles, tm, PEER_SEL)
    return pl.pallas_call(
        _topk_kernel,
        grid=(n_tiles, PEER_HEADS // hps),
        in_specs=[pl.BlockSpec((tm, hps * PEER_QDIM), lambda i, h: (i, h)), _resident(sk_bf16.shape)],
        out_specs=[per_tile, per_tile, per_step, per_step],
        out_shape=[jax.ShapeDtypeStruct(token_major, jnp.int32), jax.ShapeDtypeStruct(token_major, jnp.int32),
                   jax.ShapeDtypeStruct(slot_major, jnp.int32), jax.ShapeDtypeStruct(slot_major, F32)],
        scratch_shapes=[pltpu.VMEM((PEER_SEL, tm), jnp.int32)] * 2
        + [pltpu.VMEM((hps, PEER_TOPK, tm), F32)] * 4 + [pltpu.VMEM((hps, PAIR_ROWS, tm), F32)] * 2
        + [pltpu.VMEM((hps, PEER_TOPK, tm), F32)] * 2,
        compiler_params=_params(("parallel", "arbitrary")),
        name="peer_topk",
    )(qp, sk_bf16)


HIGH_HALF = -65536


def _pack_experts(tbl):
    b = lax.bitcast_convert_type(tbl.astype(BF16), jnp.uint16).astype(jnp.uint32)
    w = (b[HALF_EXPERTS:] << 16) | b[:HALF_EXPERTS]
    return lax.bitcast_convert_type(w, jnp.int32).reshape(HALF_EXPERTS * SUBLANES, LANES)


def _table_tile(tbl_ref, offset):
    return tbl_ref[pl.ds(pl.multiple_of(offset, SUBLANES), SUBLANES), :]


def _expert_row(tbl_ref, offset, shift):
    return lax.bitcast_convert_type((_table_tile(tbl_ref, offset) << shift) & HIGH_HALF, F32)


def _sublane_sums(parts, masks):
    shift = SUBLANES // 2
    for mask in masks:
        half = len(parts) // 2
        nxt = []
        for j in range(half):
            a, b = parts[j], parts[j + half]
            nxt.append(jnp.where(mask, a, pltpu.roll(b, shift, 0)) + jnp.where(mask, pltpu.roll(a, SUBLANES - shift, 0), b))
        parts = nxt
        shift //= 2
    return parts[0]


PEER_GROUP = SUBLANES
GROUPS_PER_TOKEN = PEER_SEL // PEER_GROUP


PACKED_ROWS = 2 * SUBLANES
V_CHUNK = 32
V_CHUNK_COLS = V_CHUNK * PACKED_ROWS
V_COLS = PEER_SEL * PACKED_ROWS
V_TOKENS_PER_STEP = 4


def _expand_matrix():
    col = np.arange(V_COLS)
    src = np.arange(2 * PEER_SEL)
    e = (src[:, None] % PEER_SEL == col[None, :] // PACKED_ROWS) & (src[:, None] // PEER_SEL == col[None, :] % 2)
    return jnp.asarray(e, BF16)


def _chunk_left(lhs, c):
    c0 = c * V_CHUNK_COLS
    half = V_CHUNK_COLS // 2
    return jnp.concatenate([lhs[:, c0:c0 + half], lhs[:, c0 + half:c0 + V_CHUNK_COLS]], axis=0)


def _chunk_weights(tiles):
    return jnp.concatenate([jnp.concatenate(tiles[:V_CHUNK // 2], axis=0),
                            jnp.concatenate(tiles[V_CHUNK // 2:], axis=0)], axis=1)


def _peer_u_kernel(row_sm, x_ref, gate_ref, sh_ref, sh_tm_ref, e_ref, d_ref, tbl_ref, pair_ref, sums_ref, fe_ref):
    tm = x_ref.shape[0]
    lane = lax.broadcasted_iota(jnp.int32, (PEER_SEL, tm), 1)
    low_tm = sh_tm_ref[0] == 16
    flags = jnp.concatenate([low_tm, jnp.logical_not(low_tm)], axis=1)
    fe_ref[...] = jnp.dot(jnp.where(flags, 1.0, 0.0).astype(BF16), e_ref[...], preferred_element_type=F32)
    blockdiag = d_ref[...]
    half_sub = V_CHUNK // 2

    def token(t):
        xb = lax.bitcast_convert_type(x_ref[t].astype(BF16).astype(F32), jnp.int32)
        x2 = pltpu.bitcast(xb | lax.shift_right_logical(xb, 16), BF16)
        lhs = (fe_ref[pl.ds(t, 1), :] * blockdiag).astype(BF16)
        for c in range(PEER_SEL // V_CHUNK):
            tiles = [pltpu.bitcast(_table_tile(tbl_ref, row_sm[0, t, c * V_CHUNK + j]), BF16) * x2
                     for j in range(V_CHUNK)]
            out = jnp.dot(_chunk_left(lhs, c), _chunk_weights(tiles), preferred_element_type=F32)
            at = pl.multiple_of(t * PEER_SEL + c * V_CHUNK, V_CHUNK)
            sums_ref[pl.ds(at, half_sub), :] = out[0:half_sub, 0:LANES]
            sums_ref[pl.ds(at + half_sub, half_sub), :] = out[half_sub:, LANES:]

    def tokens(i, carry):
        for j in range(V_TOKENS_PER_STEP):
            token(i * V_TOKENS_PER_STEP + j)
        return carry

    lax.fori_loop(0, tm // V_TOKENS_PER_STEP, tokens, 0)

    def finish(i, act):
        for j in range(SUBLANES):
            t = i * SUBLANES + j
            rows = sums_ref[pl.ds(pl.multiple_of(t * PEER_SEL, PEER_SEL), PEER_SEL), :]
            dots = jnp.sum(rows, axis=-1, keepdims=True)
            act = jnp.where(lane == t, dots, act)
        return act

    act = lax.fori_loop(0, tm // SUBLANES, finish, jnp.zeros((PEER_SEL, tm), F32))
    coef = gate_ref[0] * jax.nn.gelu(act)
    low = sh_ref[0] == 16
    pair_ref[:, 0:PEER_SEL] = jnp.where(low, coef, 0.0).T
    pair_ref[:, PEER_SEL:] = jnp.where(low, 0.0, coef).T


def _peer_v_kernel(row_sm, c2_ref, h_ref, g_ref, e_ref, d_ref, tbl_ref, y_ref, ce_ref):
    tm = h_ref.shape[0]
    c2 = c2_ref[...]
    c_hi = c2.astype(BF16)
    c_lo = (c2 - c_hi.astype(F32)).astype(BF16)
    ce_ref[0] = jnp.dot(c_hi, e_ref[...], preferred_element_type=F32)
    ce_ref[1] = jnp.dot(c_lo, e_ref[...], preferred_element_type=F32)
    diag = d_ref[...]

    def token(t):
        lhs = jnp.concatenate([ce_ref[0, pl.ds(t, 1), :] * diag, ce_ref[1, pl.ds(t, 1), :] * diag],
                              axis=0).astype(BF16)
        acc = None
        for c in range(PEER_SEL // V_CHUNK):
            tiles = [pltpu.bitcast(_table_tile(tbl_ref, row_sm[0, t, c * V_CHUNK + j]), BF16) for j in range(V_CHUNK)]
            out = jnp.dot(_chunk_left(lhs, c), _chunk_weights(tiles), preferred_element_type=F32)
            acc = out if acc is None else acc + out
        first = acc[0:SUBLANES, 0:LANES] + acc[SUBLANES:2 * SUBLANES, 0:LANES]
        second = acc[2 * SUBLANES:3 * SUBLANES, LANES:] + acc[3 * SUBLANES:, LANES:]
        y_ref[t] = h_ref[t] + (first + second)

    def tokens(i, carry):
        for j in range(V_TOKENS_PER_STEP):
            token(i * V_TOKENS_PER_STEP + j)
        return carry

    lax.fori_loop(0, tm // V_TOKENS_PER_STEP, tokens, 0)
    h = y_ref[...]
    ms = jnp.sum(jnp.sum(h * h, axis=2, keepdims=True), axis=1, keepdims=True) * (1.0 / D_MODEL)
    y_ref[...] = h * lax.rsqrt(ms + NORM_EPS) * g_ref[...]


def _smem_tile(tm):
    return pl.BlockSpec((1, tm, PEER_SEL), lambda i: (i, 0, 0), memory_space=pltpu.SMEM)


def _peer_u(rows_tm, shifts_tm, gates, shifts, u3, tbl, tm):
    n_tiles = gates.shape[0]
    vm = pl.BlockSpec((1, PEER_SEL, tm), lambda i: (i, 0, 0))
    expand = _expand_matrix()
    col = np.arange(V_COLS)
    blockdiag = jnp.asarray((col[None, :] // PACKED_ROWS) % (V_CHUNK // 2) == np.arange(V_CHUNK // 2)[:, None], F32)
    return pl.pallas_call(
        _peer_u_kernel,
        grid=(n_tiles,),
        in_specs=[_smem_tile(tm), pl.BlockSpec((tm, SUBLANES, LANES), lambda i: (i, 0, 0)), vm, vm,
                  pl.BlockSpec((1, tm, PEER_SEL), lambda i: (i, 0, 0)),
                  _resident(expand.shape), _resident(blockdiag.shape), _resident(tbl.shape)],
        out_specs=pl.BlockSpec((tm, 2 * PEER_SEL), lambda i: (i, 0)),
        out_shape=jax.ShapeDtypeStruct((n_tiles * tm, 2 * PEER_SEL), F32),
        scratch_shapes=[pltpu.VMEM((tm * PEER_SEL, LANES), F32), pltpu.VMEM((tm, V_COLS), F32)],
        compiler_params=_params(("arbitrary",)),
        name="peer_u",
    )(rows_tm, u3, gates, shifts, shifts_tm, expand, blockdiag, tbl)


def _peer_v(rows, coef_pairs, h3, g_final, tbl, tm):
    n_tiles = h3.shape[0] // tm
    tok = pl.BlockSpec((tm, SUBLANES, LANES), lambda i: (i, 0, 0))
    expand = _expand_matrix()
    diag = jnp.asarray((np.arange(V_COLS)[None, :] % PACKED_ROWS) // 2 == np.arange(SUBLANES)[:, None], F32)
    return pl.pallas_call(
        _peer_v_kernel,
        grid=(n_tiles,),
        in_specs=[_smem_tile(tm), pl.BlockSpec((tm, 2 * PEER_SEL), lambda i: (i, 0)), tok,
                  _resident(g_final.shape), _resident(expand.shape), _resident(diag.shape), _resident(tbl.shape)],
        out_specs=tok,
        out_shape=jax.ShapeDtypeStruct(h3.shape, F32),
        scratch_shapes=[pltpu.VMEM((2, tm, V_COLS), F32)],
        compiler_params=_params(("arbitrary",)),
        name="peer_v",
    )(rows, coef_pairs, h3, g_final, expand, diag, tbl)


PROJ_TILE = 512
MERGE_TILE = 256
PEER_TILE = 128


def _trunk(x, p):
    bn, s_len, d = x.shape
    n_tok = bn * s_len
    x2 = x.reshape(n_tok, d)
    qa, ka, va, qb, kb, vb, ga, gb = _inproj(x2, p["norm_mix"], p["w_in"], PROJ_TILE)
    seq = lambda t: t.reshape(bn, s_len, t.shape[-1])
    dil_outs = []
    for group, (window, dilation) in enumerate(DIL_GROUPS):
        assert window // (2 * dilation) == RAD
        sub_len = s_len // dilation
        assert s_len % dilation == 0 and sub_len % RAD == 0
        tq = min(256, sub_len)
        assert sub_len % tq == 0
        bias = _dilated_bias(p["t5_table"], group, dilation, tq)
        o, l = _dilated_group(seq(qa), seq(ka), seq(va), bias, group, dilation, tq)
        dil_outs.append((o.reshape(n_tok, A_OUT_WIDTH), l.reshape(n_tok, A_OUT_WIDTH)))
    assert s_len % NA_TILE == 0 and s_len // GRID_W >= NB_ROWS
    ob = _neighbourhood(seq(qb), seq(kb), seq(vb), p["na_bias"]).reshape(n_tok, B_WIDTH)
    h, u, qp = _merge(x2, dil_outs, ob, ga, gb, p["w_proj_a"], p["w_proj_b"], p["w_out"], p["norm_ffn"],
                      p["w_query"], MERGE_TILE)
    rows_tm, shifts_tm, shifts, gates = _topk(qp, p["sub_keys"], PEER_TILE)
    tile3 = lambda t: t.reshape(n_tok, SUBLANES, LANES)
    coef_pairs = _peer_u(rows_tm, shifts_tm, gates, shifts, tile3(u), p["expert_u"], PEER_TILE)
    y = _peer_v(rows_tm, coef_pairs, tile3(h), p["norm_final"], p["expert_v"], PEER_TILE)
    return y.reshape(bn, s_len, d)


def kernel(x_prompt, x_sample, norm_mix, w_in, w_proj_a, w_proj_b, w_out, rpb, norm_ffn, w_query, sub_keys,
           expert_u, expert_v, t5_table, norm_final):
    assert norm_mix.shape[0] == 1, "single-layer model"
    p = {
        "norm_mix": norm_mix[0].reshape(1, D_MODEL).astype(F32),
        "w_in": w_in[0].astype(BF16),
        "w_proj_a": w_proj_a[0].astype(BF16),
        "w_proj_b": w_proj_b[0].astype(BF16),
        "w_out": w_out[0].astype(BF16),
        "na_bias": _na_bias(rpb[0]),
        "norm_ffn": norm_ffn[0].reshape(1, D_MODEL).astype(F32),
        "w_query": w_query[0].astype(BF16),
        "sub_keys": sub_keys[0].astype(BF16),
        "expert_u": _pack_experts(expert_u[0]),
        "expert_v": _pack_experts(expert_v[0]),
        "t5_table": t5_table,
        "norm_final": norm_final.reshape(SUBLANES, LANES).astype(F32),
    }
    return (_trunk(x_prompt, p), _trunk(x_sample, p))
```

```python
import functools
import math

import numpy as np
import jax
import jax.numpy as jnp
from jax import lax
from jax.experimental import pallas as pl
from jax.experimental.pallas import tpu as pltpu

D_MODEL = 1024
HEAD_DIM = 64
DIL_GROUPS = ((128, 1), (512, 4), (2048, 16))
A_HEADS_PER_GROUP = 4
A_HEADS = A_HEADS_PER_GROUP * len(DIL_GROUPS)
A_WIDTH = A_HEADS * HEAD_DIM
A_OUT_WIDTH = A_HEADS_PER_GROUP * HEAD_DIM
T5_BUCKETS = 32
T5_MAX_DIST = 1024
GRID_W = 64
B_HEADS = 8
B_WIDTH = B_HEADS * HEAD_DIM
NB_ROWS = 8
NB_COLS = 16
PEER_HEADS = 8
PEER_NKEYS = 128
PEER_EXPERTS = PEER_NKEYS * PEER_NKEYS
PEER_QDIM = 256
PEER_HALF = PEER_QDIM // 2
PEER_TOPK = 16
PEER_SEL = PEER_HEADS * PEER_TOPK
NORM_EPS = 1e-6
NEG_INF = -1e30

RAD = 64
SUBLANES = 8
LANES = 128
HALF_EXPERTS = PEER_EXPERTS // 2
VMEM_LIMIT = 48 * 1024 * 1024

F32 = jnp.float32
BF16 = jnp.bfloat16


def _resident(shape):
    nd = len(shape)
    return pl.BlockSpec(shape, lambda *_: (0,) * nd, pipeline_mode=pl.Buffered(1))


def _params(sem):
    return pltpu.CompilerParams(dimension_semantics=sem, vmem_limit_bytes=VMEM_LIMIT)


IN_SPLITS = (A_WIDTH, A_WIDTH, A_WIDTH, B_WIDTH, B_WIDTH, B_WIDTH, D_MODEL, D_MODEL)
QK_SCALE = HEAD_DIM ** -0.5


def _arrange_w_in(w_in):
    qa, ka, va, qb, rest = jnp.split(w_in, [A_WIDTH, 2 * A_WIDTH, 3 * A_WIDTH, 3 * A_WIDTH + B_WIDTH], axis=1)
    groups = []
    for g in range(len(DIL_GROUPS)):
        cs = slice(g * A_OUT_WIDTH, (g + 1) * A_OUT_WIDTH)
        groups += [qa[:, cs] * QK_SCALE, ka[:, cs], va[:, cs]]
    return jnp.concatenate(groups + [qb * QK_SCALE, rest], axis=1).astype(BF16)


def _rms(x, g):
    return x * lax.rsqrt(jnp.mean(x * x, axis=-1, keepdims=True) + NORM_EPS) * g


def _inproj_kernel(x_ref, g_ref, w_ref, *out_refs):
    u = _rms(x_ref[...], g_ref[...]).astype(BF16)
    off = 0
    for o_ref, width in zip(out_refs, IN_SPLITS):
        o_ref[...] = jnp.dot(u, w_ref[:, off:off + width], preferred_element_type=F32).astype(o_ref.dtype)
        off += width


def _inproj(x2, g, w_bf16, tm):
    n_tok = x2.shape[0]
    return pl.pallas_call(
        _inproj_kernel,
        grid=(n_tok // tm,),
        in_specs=[pl.BlockSpec((tm, D_MODEL), lambda i: (i, 0)),
                  _resident((1, D_MODEL)),
                  _resident(w_bf16.shape)],
        out_specs=[pl.BlockSpec((tm, w), lambda i: (i, 0)) for w in IN_SPLITS],
        out_shape=[jax.ShapeDtypeStruct((n_tok, w), BF16) for w in IN_SPLITS],
        compiler_params=_params(("parallel",)),
        name="inproj",
    )(x2, g, w_bf16)


def _t5_bucket(rel):
    n = -rel
    nb = T5_BUCKETS // 2
    ret = (n < 0).astype(np.int32) * nb
    n = np.abs(n)
    max_exact = nb // 2
    large = max_exact + (np.log(np.maximum(n, 1) / max_exact) / math.log(T5_MAX_DIST / max_exact)
                         * (nb - max_exact)).astype(np.int32)
    large = np.minimum(large, nb - 1)
    return (ret + np.where(n < max_exact, n, large)).astype(np.int32)


def _dilated_bias(t5_table, group, dilation, tq):
    rel = np.arange(tq + 2 * RAD)[None, :] - RAD - np.arange(tq)[:, None]
    tab = t5_table[:, group * A_HEADS_PER_GROUP:(group + 1) * A_HEADS_PER_GROUP].astype(F32)
    in_band = np.abs(rel) <= RAD
    bucket = np.where(in_band, _t5_bucket(rel * dilation), -1)
    bias = jnp.full((A_HEADS_PER_GROUP,) + rel.shape, NEG_INF, F32)
    for b in np.unique(bucket[in_band]):
        bias = jnp.where(jnp.asarray(bucket == b)[None], tab[int(b)][:, None, None], bias)
    return bias


def _dilated_kernel(q_ref, kp_ref, kc_ref, kn_ref, vp_ref, vc_ref, vn_ref, bias_ref, o_ref, l_ref, *, tq, n_tiles):
    i = pl.program_id(2)
    q = q_ref[0]
    k_all = jnp.concatenate([kp_ref[0], kc_ref[0], kn_ref[0]], axis=0)
    v_all = jnp.concatenate([vp_ref[0], vc_ref[0], vn_ref[0]], axis=0)
    nk = tq + 2 * RAD
    col = lax.broadcasted_iota(jnp.int32, (tq, nk), 1)
    first_valid = jnp.where(i == 0, RAD, 0)
    end_valid = jnp.where(i == n_tiles - 1, tq + RAD, nk)
    valid = (col >= first_valid) & (col < end_valid)
    lane = lax.broadcasted_iota(jnp.int32, (tq, A_OUT_WIDTH), 1)
    o_acc = jnp.zeros((tq, A_OUT_WIDTH), F32)
    l_acc = jnp.zeros((tq, A_OUT_WIDTH), F32)
    for h in range(A_HEADS_PER_GROUP):
        head = (lane >= h * HEAD_DIM) & (lane < (h + 1) * HEAD_DIM)
        qh = jnp.where(head, q, jnp.zeros_like(q))
        s = lax.dot_general(qh, k_all, (((1,), (1,)), ((), ())), preferred_element_type=F32)
        s = jnp.where(valid, s + bias_ref[h], NEG_INF)
        m = jnp.max(s, axis=-1, keepdims=True)
        p = jnp.exp(s - m)
        den = jnp.sum(p, axis=-1, keepdims=True)
        o = jnp.dot(p.astype(BF16), v_all, preferred_element_type=F32) / den
        lse = m + jnp.log(den)
        o_acc = jnp.where(head, o, o_acc)
        l_acc = jnp.where(head, lse, l_acc)
    o_ref[0] = o_acc
    l_ref[0] = l_acc


def _dilated_group(qkv, bias, group, dilation, tq):
    bn, s_len, _ = qkv.shape
    sub_len = s_len // dilation
    n_tiles = sub_len // tq
    blk = tq // RAD
    n_rad = sub_len // RAD
    view = qkv.reshape(bn, sub_len, dilation * A_WIDTH)

    def specs(part):
        cur = pl.BlockSpec((1, tq, A_OUT_WIDTH), lambda b, r, i: (b, i, r * 3 + part))
        prev = pl.BlockSpec((1, RAD, A_OUT_WIDTH), lambda b, r, i: (b, jnp.maximum(i * blk - 1, 0), r * 3 + part))
        nxt = pl.BlockSpec((1, RAD, A_OUT_WIDTH),
                           lambda b, r, i: (b, jnp.minimum((i + 1) * blk, n_rad - 1), r * 3 + part))
        return cur, prev, nxt

    out = pl.BlockSpec((1, tq, A_OUT_WIDTH), lambda b, r, i: (b, i, r))
    (q_cur, _, _), (k_cur, k_prev, k_next), (v_cur, v_prev, v_next) = specs(0), specs(1), specs(2)
    o, l = pl.pallas_call(
        functools.partial(_dilated_kernel, tq=tq, n_tiles=n_tiles),
        grid=(bn, dilation, n_tiles),
        in_specs=[q_cur, k_prev, k_cur, k_next, v_prev, v_cur, v_next, _resident(bias.shape)],
        out_specs=[out, out],
        out_shape=[jax.ShapeDtypeStruct((bn, sub_len, dilation * A_OUT_WIDTH), F32)] * 2,
        compiler_params=_params(("parallel", "parallel", "parallel")),
        name=f"dilated{group}",
    )(view, view, view, view, view, view, view, bias)
    return o.reshape(bn, s_len, A_OUT_WIDTH), l.reshape(bn, s_len, A_OUT_WIDTH)


NA_TILE_ROWS = 8
NA_TILE = NA_TILE_ROWS * GRID_W
NA_KEYS = NB_ROWS * GRID_W


def _na_bias(rpb):
    cols = np.arange(GRID_W)
    col_start = np.clip(cols - NB_COLS // 2, 0, GRID_W - NB_COLS)
    col_valid = (cols[None, :] >= col_start[:, None]) & (cols[None, :] < col_start[:, None] + NB_COLS)
    dc = np.clip(cols[None, :] - cols[:, None] + NB_COLS - 1, 0, 2 * NB_COLS - 2)
    rpb = rpb.astype(F32)
    variants = []
    for v in range(NB_ROWS):
        dr = np.clip(np.arange(NB_ROWS) + v, 0, 2 * NB_ROWS - 2)
        tab = rpb[:, dr][:, :, dc]
        tab = jnp.where(jnp.asarray(col_valid)[None, None], tab, NEG_INF)
        variants.append(jnp.transpose(tab, (0, 2, 1, 3)).reshape(B_HEADS // 2, 2 * GRID_W, NA_KEYS))
    return jnp.stack(variants, axis=0)


def _na_kernel(q_ref, kp_ref, kc_ref, kn_ref, vp_ref, vc_ref, vn_ref, bias_ref, o_ref, kwin, vwin, *, rows):
    i = pl.program_id(1)
    kwin[0:NA_TILE] = kp_ref[0]
    kwin[NA_TILE:2 * NA_TILE] = kc_ref[0]
    kwin[2 * NA_TILE:3 * NA_TILE] = kn_ref[0]
    vwin[0:NA_TILE] = vp_ref[0]
    vwin[NA_TILE:2 * NA_TILE] = vc_ref[0]
    vwin[2 * NA_TILE:3 * NA_TILE] = vn_ref[0]
    pair = 2 * HEAD_DIM
    row_id = lax.broadcasted_iota(jnp.int32, (2 * GRID_W, pair), 0)
    lane_id = lax.broadcasted_iota(jnp.int32, (2 * GRID_W, pair), 1)
    own = (row_id < GRID_W) == (lane_id < HEAD_DIM)
    low = lax.broadcasted_iota(jnp.int32, (GRID_W, pair), 1) < HEAD_DIM

    def query_row(a, carry):
        r = i * NA_TILE_ROWS + a
        rs = jnp.clip(r - NB_ROWS // 2, 0, rows - NB_ROWS)
        start = pl.multiple_of((rs - (i - 1) * NA_TILE_ROWS) * GRID_W, GRID_W)
        variant = rs - r + NB_ROWS - 1
        q_rows = pl.ds(pl.multiple_of(a * GRID_W, GRID_W), GRID_W)
        scores = []
        for hp in range(B_HEADS // 2):
            cs = slice(hp * pair, (hp + 1) * pair)
            q2 = q_ref[0, q_rows, cs]
            qs = jnp.where(own, jnp.concatenate([q2, q2], axis=0), jnp.zeros((2 * GRID_W, pair), q2.dtype))
            s = lax.dot_general(qs, kwin[pl.ds(start, NA_KEYS), cs], (((1,), (1,)), ((), ())),
                                preferred_element_type=F32)
            scores.append(s + bias_ref[variant, hp])
        probs = []
        for s in scores:
            p = jnp.exp(s - jnp.max(s, axis=-1, keepdims=True))
            probs.append((p, jnp.sum(p, axis=-1, keepdims=True)))
        outs = []
        for hp, (p, den) in enumerate(probs):
            cs = slice(hp * pair, (hp + 1) * pair)
            o = jnp.dot(p.astype(BF16), vwin[pl.ds(start, NA_KEYS), cs], preferred_element_type=F32) / den
            outs.append(jnp.where(low, o[0:GRID_W], o[GRID_W:]))
        o_ref[0, q_rows, :] = jnp.concatenate(outs, axis=1).astype(o_ref.dtype)
        return carry

    lax.fori_loop(0, NA_TILE_ROWS, query_row, 0)


def _neighbourhood(qb, kb, vb, bias):
    bn, s_len, _ = qb.shape
    rows = s_len // GRID_W
    n_tiles = rows // NA_TILE_ROWS
    cur = pl.BlockSpec((1, NA_TILE, B_WIDTH), lambda b, i: (b, i, 0))
    prev = pl.BlockSpec((1, NA_TILE, B_WIDTH), lambda b, i: (b, jnp.maximum(i - 1, 0), 0))
    nxt = pl.BlockSpec((1, NA_TILE, B_WIDTH), lambda b, i: (b, jnp.minimum(i + 1, n_tiles - 1), 0))
    return pl.pallas_call(
        functools.partial(_na_kernel, rows=rows),
        grid=(bn, n_tiles),
        in_specs=[cur, prev, cur, nxt, prev, cur, nxt, _resident(bias.shape)],
        out_specs=cur,
        out_shape=jax.ShapeDtypeStruct((bn, s_len, B_WIDTH), BF16),
        scratch_shapes=[pltpu.VMEM((3 * NA_TILE, B_WIDTH), BF16)] * 2,
        compiler_params=_params(("parallel", "parallel")),
        name="neighbourhood",
    )(qb, kb, kb, kb, vb, vb, vb, bias)


def _merge_kernel(x_ref, o1_ref, l1_ref, o2_ref, l2_ref, o3_ref, l3_ref, ob_ref, ga_ref, gb_ref,
                  wpa_ref, wpb_ref, wout_ref, gffn_ref, wq_ref, h_ref, u_ref, qp_ref):
    l1, l2, l3 = l1_ref[...], l2_ref[...], l3_ref[...]
    m = jnp.maximum(jnp.maximum(l1, l2), l3)
    w1, w2, w3 = jnp.exp(l1 - m), jnp.exp(l2 - m), jnp.exp(l3 - m)
    oa = (w1 * o1_ref[...] + w2 * o2_ref[...] + w3 * o3_ref[...]) / (w1 + w2 + w3)
    pa = jnp.dot(oa.astype(BF16), wpa_ref[...], preferred_element_type=F32)
    pb = jnp.dot(ob_ref[...], wpb_ref[...], preferred_element_type=F32)
    merged = jax.nn.sigmoid(ga_ref[...].astype(F32)) * pa + jax.nn.sigmoid(gb_ref[...].astype(F32)) * pb
    h = x_ref[...] + jnp.dot(merged.astype(BF16), wout_ref[...], preferred_element_type=F32)
    u = _rms(h, gffn_ref[...])
    h_ref[...] = h
    u_ref[...] = u
    qp_ref[...] = jnp.dot(u.astype(BF16), wq_ref[...], preferred_element_type=F32).astype(qp_ref.dtype)


def _merge(x2, dil_outs, ob, ga, gb, wpa, wpb, wout, gffn, wq, tm):
    n_tok = x2.shape[0]
    tok = lambda w: pl.BlockSpec((tm, w), lambda i: (i, 0))
    ins = [x2]
    specs = [tok(D_MODEL)]
    for o, l in dil_outs:
        ins += [o, l]
        specs += [tok(A_OUT_WIDTH), tok(A_OUT_WIDTH)]
    ins += [ob, ga, gb, wpa, wpb, wout, gffn, wq]
    specs += [tok(B_WIDTH), tok(D_MODEL), tok(D_MODEL), _resident(wpa.shape), _resident(wpb.shape),
              _resident(wout.shape), _resident(gffn.shape), _resident(wq.shape)]
    qw = PEER_HEADS * PEER_QDIM
    return pl.pallas_call(
        _merge_kernel,
        grid=(n_tok // tm,),
        in_specs=specs,
        out_specs=[tok(D_MODEL), tok(D_MODEL), tok(qw)],
        out_shape=[jax.ShapeDtypeStruct((n_tok, D_MODEL), F32), jax.ShapeDtypeStruct((n_tok, D_MODEL), F32),
                   jax.ShapeDtypeStruct((n_tok, qw), BF16)],
        compiler_params=_params(("parallel",)),
        name="merge",
    )(*ins)


def _take_top(problems, count):
    state = [s for s, _, _, _ in problems]
    for j in range(count):
        for i, (_, val_ref, pick_ref, payload) in enumerate(problems):
            s = state[i]
            iota = lax.broadcasted_iota(jnp.int32, s.shape, 0).astype(F32)
            m = jnp.max(s, axis=0, keepdims=True)
            pos = jnp.min(jnp.where(s == m, iota, float(s.shape[0])), axis=0, keepdims=True)
            hit = iota == pos
            val_ref[j:j + 1, :] = m
            pick_ref[j:j + 1, :] = (pos if payload is None
                                    else jnp.max(jnp.where(hit, payload, -1.0), axis=0, keepdims=True))
            state[i] = jnp.where(hit, -jnp.inf, s)


PAIR_COUNTS = tuple(PEER_TOPK // (a + 1) for a in range(PEER_TOPK))
N_PAIRS = sum(PAIR_COUNTS)
PAIR_ROWS = -(-N_PAIRS // SUBLANES) * SUBLANES


TOPK_HEADS_PER_STEP = 4


def _topk_kernel(qp_ref, sk_ref, row_tm_ref, sh_tm_ref, sh_ref, gate_ref,
                 row_scr, sh_scr, va_ref, ia_ref, vb_ref, ib_ref, cand_ref, cidx_ref, best_ref, pick_ref):
    tm = qp_ref.shape[0]
    step = pl.program_id(1)
    stage1 = []
    for hh in range(TOPK_HEADS_PER_STEP):
        for p, (val_ref, idx_ref) in enumerate(((va_ref, ia_ref), (vb_ref, ib_ref))):
            c0 = hh * PEER_QDIM + p * PEER_HALF
            q = qp_ref[:, c0:c0 + PEER_HALF]
            s = lax.dot_general(sk_ref[p], q, (((1,), (1,)), ((), ())), preferred_element_type=F32)
            stage1.append((s, val_ref.at[hh], idx_ref.at[hh], None))
    _take_top(stage1, PEER_TOPK)
    stage2 = []
    for hh in range(TOPK_HEADS_PER_STEP):
        cand_ref[hh, N_PAIRS:, :] = jnp.full((PAIR_ROWS - N_PAIRS, tm), -jnp.inf, F32)
        cidx_ref[hh, N_PAIRS:, :] = jnp.zeros((PAIR_ROWS - N_PAIRS, tm), F32)
        at = 0
        for a, count in enumerate(PAIR_COUNTS):
            cand_ref[hh, at:at + count, :] = va_ref[hh, a:a + 1, :] + vb_ref[hh, 0:count, :]
            cidx_ref[hh, at:at + count, :] = ia_ref[hh, a:a + 1, :] * PEER_NKEYS + ib_ref[hh, 0:count, :]
            at += count
        stage2.append((cand_ref[hh], best_ref.at[hh], pick_ref.at[hh], cidx_ref[hh]))
    _take_top(stage2, PEER_TOPK)
    rows_per_step = TOPK_HEADS_PER_STEP * PEER_TOPK
    best = best_ref[...]
    e = jnp.exp(best - best[:, 0:1])
    gate_ref[0] = (e / jnp.sum(e, axis=1, keepdims=True)).reshape(rows_per_step, tm)
    experts = pick_ref[...].astype(jnp.int32).reshape(rows_per_step, tm)
    shift = jnp.where(experts < HALF_EXPERTS, 16, 0)
    sh_ref[0] = shift
    slots = pl.ds(pl.multiple_of(step * rows_per_step, rows_per_step), rows_per_step)
    row_scr[slots, :] = (experts & (HALF_EXPERTS - 1)) * SUBLANES
    sh_scr[slots, :] = shift

    @pl.when(step == PEER_HEADS // TOPK_HEADS_PER_STEP - 1)
    def _():
        row_tm_ref[0] = row_scr[...].T
        sh_tm_ref[0] = sh_scr[...].T


def _topk(qp, sk_bf16, tm):
    n_tok = qp.shape[0]
    n_tiles = n_tok // tm
    hps = TOPK_HEADS_PER_STEP
    per_step = pl.BlockSpec((1, hps * PEER_TOPK, tm), lambda i, h: (i, h, 0))
    per_tile = pl.BlockSpec((1, tm, PEER_SEL), lambda i, h: (i, 0, 0))
    slot_major = (n_tiles, PEER_SEL, tm)
    token_major = (n_tiles, tm, PEER_SEL)
    return pl.pallas_call(
        _topk_kernel,
        grid=(n_tiles, PEER_HEADS // hps),
        in_specs=[pl.BlockSpec((tm, hps * PEER_QDIM), lambda i, h: (i, h)), _resident(sk_bf16.shape)],
        out_specs=[per_tile, per_tile, per_step, per_step],
        out_shape=[jax.ShapeDtypeStruct(token_major, jnp.int32), jax.ShapeDtypeStruct(token_major, jnp.int32),
                   jax.ShapeDtypeStruct(slot_major, jnp.int32), jax.ShapeDtypeStruct(slot_major, F32)],
        scratch_shapes=[pltpu.VMEM((PEER_SEL, tm), jnp.int32)] * 2
        + [pltpu.VMEM((hps, PEER_TOPK, tm), F32)] * 4 + [pltpu.VMEM((hps, PAIR_ROWS, tm), F32)] * 2
        + [pltpu.VMEM((hps, PEER_TOPK, tm), F32)] * 2,
        compiler_params=_params(("parallel", "arbitrary")),
        name="peer_topk",
    )(qp, sk_bf16)


def _pack_experts(tbl):
    b = lax.bitcast_convert_type(tbl.astype(BF16), jnp.uint16).astype(jnp.uint32)
    w = (b[HALF_EXPERTS:] << 16) | b[:HALF_EXPERTS]
    return lax.bitcast_convert_type(w, jnp.int32).reshape(HALF_EXPERTS * SUBLANES, LANES)


def _table_tile(tbl_ref, offset):
    return tbl_ref[pl.ds(pl.multiple_of(offset, SUBLANES), SUBLANES), :]


PACKED_ROWS = 2 * SUBLANES
V_CHUNK = 32
V_CHUNK_COLS = V_CHUNK * PACKED_ROWS
V_COLS = PEER_SEL * PACKED_ROWS
V_TOKENS_PER_STEP = 4


def _expand_matrix():
    col = np.arange(V_COLS)
    src = np.arange(2 * PEER_SEL)
    e = (src[:, None] % PEER_SEL == col[None, :] // PACKED_ROWS) & (src[:, None] // PEER_SEL == col[None, :] % 2)
    return jnp.asarray(e, BF16)


def _chunk_left(lhs, c):
    c0 = c * V_CHUNK_COLS
    half = V_CHUNK_COLS // 2
    return jnp.concatenate([lhs[:, c0:c0 + half], lhs[:, c0 + half:c0 + V_CHUNK_COLS]], axis=0)


def _chunk_weights(tiles):
    return jnp.concatenate([jnp.concatenate(tiles[:V_CHUNK // 2], axis=0),
                            jnp.concatenate(tiles[V_CHUNK // 2:], axis=0)], axis=1)


def _peer_u_kernel(row_sm, x_ref, gate_ref, sh_ref, sh_tm_ref, e_ref, d_ref, tbl_ref, pair_ref, sums_ref, fe_ref):
    tm = x_ref.shape[0]
    lane = lax.broadcasted_iota(jnp.int32, (PEER_SEL, tm), 1)
    low_tm = sh_tm_ref[0] == 16
    flags = jnp.concatenate([low_tm, jnp.logical_not(low_tm)], axis=1)
    fe_ref[...] = jnp.dot(jnp.where(flags, 1.0, 0.0).astype(BF16), e_ref[...], preferred_element_type=F32)
    blockdiag = d_ref[...]
    half_sub = V_CHUNK // 2

    def token(t):
        xb = lax.bitcast_convert_type(x_ref[t].astype(BF16).astype(F32), jnp.int32)
        x2 = pltpu.bitcast(xb | lax.shift_right_logical(xb, 16), BF16)
        lhs = (fe_ref[pl.ds(t, 1), :] * blockdiag).astype(BF16)
        for c in range(PEER_SEL // V_CHUNK):
            tiles = [pltpu.bitcast(_table_tile(tbl_ref, row_sm[0, t, c * V_CHUNK + j]), BF16) * x2
                     for j in range(V_CHUNK)]
            out = jnp.dot(_chunk_left(lhs, c), _chunk_weights(tiles), preferred_element_type=F32)
            at = pl.multiple_of(t * PEER_SEL + c * V_CHUNK, V_CHUNK)
            sums_ref[pl.ds(at, half_sub), :] = out[0:half_sub, 0:LANES]
            sums_ref[pl.ds(at + half_sub, half_sub), :] = out[half_sub:, LANES:]

    def tokens(i, carry):
        for j in range(V_TOKENS_PER_STEP):
            token(i * V_TOKENS_PER_STEP + j)
        return carry

    lax.fori_loop(0, tm // V_TOKENS_PER_STEP, tokens, 0)

    def finish(i, act):
        for j in range(SUBLANES):
            t = i * SUBLANES + j
            rows = sums_ref[pl.ds(pl.multiple_of(t * PEER_SEL, PEER_SEL), PEER_SEL), :]
            dots = jnp.sum(rows, axis=-1, keepdims=True)
            act = jnp.where(lane == t, dots, act)
        return act

    act = lax.fori_loop(0, tm // SUBLANES, finish, jnp.zeros((PEER_SEL, tm), F32))
    coef = gate_ref[0] * jax.nn.gelu(act)
    low = sh_ref[0] == 16
    pair_ref[:, 0:PEER_SEL] = jnp.where(low, coef, 0.0).T
    pair_ref[:, PEER_SEL:] = jnp.where(low, 0.0, coef).T


def _peer_v_kernel(row_sm, c2_ref, h_ref, g_ref, e_ref, d_ref, tbl_ref, y_ref, ce_ref):
    tm = h_ref.shape[0]
    c2 = c2_ref[...]
    c_hi = c2.astype(BF16)
    c_lo = (c2 - c_hi.astype(F32)).astype(BF16)
    ce_ref[0] = jnp.dot(c_hi, e_ref[...], preferred_element_type=F32)
    ce_ref[1] = jnp.dot(c_lo, e_ref[...], preferred_element_type=F32)
    diag = d_ref[...]

    def token(t):
        lhs = jnp.concatenate([ce_ref[0, pl.ds(t, 1), :] * diag, ce_ref[1, pl.ds(t, 1), :] * diag],
                              axis=0).astype(BF16)
        acc = None
        for c in range(PEER_SEL // V_CHUNK):
            tiles = [pltpu.bitcast(_table_tile(tbl_ref, row_sm[0, t, c * V_CHUNK + j]), BF16) for j in range(V_CHUNK)]
            out = jnp.dot(_chunk_left(lhs, c), _chunk_weights(tiles), preferred_element_type=F32)
            acc = out if acc is None else acc + out
        first = acc[0:SUBLANES, 0:LANES] + acc[SUBLANES:2 * SUBLANES, 0:LANES]
        second = acc[2 * SUBLANES:3 * SUBLANES, LANES:] + acc[3 * SUBLANES:, LANES:]
        y_ref[t] = h_ref[t] + (first + second)

    def tokens(i, carry):
        for j in range(V_TOKENS_PER_STEP):
            token(i * V_TOKENS_PER_STEP + j)
        return carry

    lax.fori_loop(0, tm // V_TOKENS_PER_STEP, tokens, 0)
    h = y_ref[...]
    ms = jnp.sum(jnp.sum(h * h, axis=2, keepdims=True), axis=1, keepdims=True) * (1.0 / D_MODEL)
    y_ref[...] = h * lax.rsqrt(ms + NORM_EPS) * g_ref[...]


def _smem_tile(tm):
    return pl.BlockSpec((1, tm, PEER_SEL), lambda i: (i, 0, 0), memory_space=pltpu.SMEM)


def _peer_u(rows_tm, shifts_tm, gates, shifts, u3, tbl, tm):
    n_tiles = gates.shape[0]
    vm = pl.BlockSpec((1, PEER_SEL, tm), lambda i: (i, 0, 0))
    expand = _expand_matrix()
    col = np.arange(V_COLS)
    blockdiag = jnp.asarray((col[None, :] // PACKED_ROWS) % (V_CHUNK // 2) == np.arange(V_CHUNK // 2)[:, None], F32)
    return pl.pallas_call(
        _peer_u_kernel,
        grid=(n_tiles,),
        in_specs=[_smem_tile(tm), pl.BlockSpec((tm, SUBLANES, LANES), lambda i: (i, 0, 0)), vm, vm,
                  pl.BlockSpec((1, tm, PEER_SEL), lambda i: (i, 0, 0)),
                  _resident(expand.shape), _resident(blockdiag.shape), _resident(tbl.shape)],
        out_specs=pl.BlockSpec((tm, 2 * PEER_SEL), lambda i: (i, 0)),
        out_shape=jax.ShapeDtypeStruct((n_tiles * tm, 2 * PEER_SEL), F32),
        scratch_shapes=[pltpu.VMEM((tm * PEER_SEL, LANES), F32), pltpu.VMEM((tm, V_COLS), F32)],
        compiler_params=_params(("arbitrary",)),
        name="peer_u",
    )(rows_tm, u3, gates, shifts, shifts_tm, expand, blockdiag, tbl)


def _peer_v(rows, coef_pairs, h3, g_final, tbl, tm):
    n_tiles = h3.shape[0] // tm
    tok = pl.BlockSpec((tm, SUBLANES, LANES), lambda i: (i, 0, 0))
    expand = _expand_matrix()
    diag = jnp.asarray((np.arange(V_COLS)[None, :] % PACKED_ROWS) // 2 == np.arange(SUBLANES)[:, None], F32)
    return pl.pallas_call(
        _peer_v_kernel,
        grid=(n_tiles,),
        in_specs=[_smem_tile(tm), pl.BlockSpec((tm, 2 * PEER_SEL), lambda i: (i, 0)), tok,
                  _resident(g_final.shape), _resident(expand.shape), _resident(diag.shape), _resident(tbl.shape)],
        out_specs=tok,
        out_shape=jax.ShapeDtypeStruct(h3.shape, F32),
        scratch_shapes=[pltpu.VMEM((2, tm, V_COLS), F32)],
        compiler_params=_params(("arbitrary",)),
        name="peer_v",
    )(rows, coef_pairs, h3, g_final, expand, diag, tbl)


PROJ_TILE = 512
MERGE_TILE = 256
PEER_TILE = 128


def _trunk(x, p):
    bn, s_len, d = x.shape
    n_tok = bn * s_len
    x2 = x.reshape(n_tok, d)
    *qkv_groups, qb, kb, vb, ga, gb = _inproj(x2, p["norm_mix"], p["w_in"], PROJ_TILE)
    seq = lambda t: t.reshape(bn, s_len, t.shape[-1])
    dil_outs = []
    for group, (window, dilation) in enumerate(DIL_GROUPS):
        assert window // (2 * dilation) == RAD
        sub_len = s_len // dilation
        assert s_len % dilation == 0 and sub_len % RAD == 0
        tq = min(256, sub_len)
        assert sub_len % tq == 0
        bias = _dilated_bias(p["t5_table"], group, dilation, tq)
        o, l = _dilated_group(seq(qkv_groups[group]), bias, group, dilation, tq)
        dil_outs.append((o.reshape(n_tok, A_OUT_WIDTH), l.reshape(n_tok, A_OUT_WIDTH)))
    assert s_len % NA_TILE == 0 and s_len // GRID_W >= NB_ROWS
    ob = _neighbourhood(seq(qb), seq(kb), seq(vb), p["na_bias"]).reshape(n_tok, B_WIDTH)
    h, u, qp = _merge(x2, dil_outs, ob, ga, gb, p["w_proj_a"], p["w_proj_b"], p["w_out"], p["norm_ffn"],
                      p["w_query"], MERGE_TILE)
    rows_tm, shifts_tm, shifts, gates = _topk(qp, p["sub_keys"], PEER_TILE)
    tile3 = lambda t: t.reshape(n_tok, SUBLANES, LANES)
    coef_pairs = _peer_u(rows_tm, shifts_tm, gates, shifts, tile3(u), p["expert_u"], PEER_TILE)
    y = _peer_v(rows_tm, coef_pairs, tile3(h), p["norm_final"], p["expert_v"], PEER_TILE)
    return y.reshape(bn, s_len, d)


def kernel(x_prompt, x_sample, norm_mix, w_in, w_proj_a, w_proj_b, w_out, rpb, norm_ffn, w_query, sub_keys,
           expert_u, expert_v, t5_table, norm_final):
    assert norm_mix.shape[0] == 1, "single-layer model"
    p = {
        "norm_mix": norm_mix[0].reshape(1, D_MODEL).astype(F32),
        "w_in": _arrange_w_in(w_in[0]),
        "w_proj_a": w_proj_a[0].astype(BF16),
        "w_proj_b": w_proj_b[0].astype(BF16),
        "w_out": w_out[0].astype(BF16),
        "na_bias": _na_bias(rpb[0]),
        "norm_ffn": norm_ffn[0].reshape(1, D_MODEL).astype(F32),
        "w_query": w_query[0].astype(BF16),
        "sub_keys": sub_keys[0].astype(BF16),
        "expert_u": _pack_experts(expert_u[0]),
        "expert_v": _pack_experts(expert_v[0]),
        "t5_table": t5_table,
        "norm_final": norm_final.reshape(SUBLANES, LANES).astype(F32),
    }
    return (_trunk(x_prompt, p), _trunk(x_sample, p))
```

```python
import functools
import math

import numpy as np
import jax
import jax.numpy as jnp
from jax import lax
from jax.experimental import pallas as pl
from jax.experimental.pallas import tpu as pltpu

D_MODEL = 1024
HEAD_DIM = 64
DIL_GROUPS = ((128, 1), (512, 4), (2048, 16))
A_HEADS_PER_GROUP = 4
A_HEADS = A_HEADS_PER_GROUP * len(DIL_GROUPS)
A_WIDTH = A_HEADS * HEAD_DIM
A_OUT_WIDTH = A_HEADS_PER_GROUP * HEAD_DIM
T5_BUCKETS = 32
T5_MAX_DIST = 1024
GRID_W = 64
B_HEADS = 8
B_WIDTH = B_HEADS * HEAD_DIM
NB_ROWS = 8
NB_COLS = 16
PEER_HEADS = 8
PEER_NKEYS = 128
PEER_EXPERTS = PEER_NKEYS * PEER_NKEYS
PEER_QDIM = 256
PEER_HALF = PEER_QDIM // 2
PEER_TOPK = 16
PEER_SEL = PEER_HEADS * PEER_TOPK
NORM_EPS = 1e-6
NEG_INF = -1e30

RAD = 64
SUBLANES = 8
LANES = 128
HALF_EXPERTS = PEER_EXPERTS // 2
VMEM_LIMIT = 48 * 1024 * 1024

F32 = jnp.float32
BF16 = jnp.bfloat16


def _resident(shape):
    nd = len(shape)
    return pl.BlockSpec(shape, lambda *_: (0,) * nd, pipeline_mode=pl.Buffered(1))


def _params(sem):
    return pltpu.CompilerParams(dimension_semantics=sem, vmem_limit_bytes=VMEM_LIMIT)


IN_SPLITS = (A_WIDTH, A_WIDTH, A_WIDTH, B_WIDTH, B_WIDTH, B_WIDTH, D_MODEL, D_MODEL)
QK_SCALE = HEAD_DIM ** -0.5


def _arrange_w_in(w_in):
    qa, ka, va, qb, rest = jnp.split(w_in, [A_WIDTH, 2 * A_WIDTH, 3 * A_WIDTH, 3 * A_WIDTH + B_WIDTH], axis=1)
    groups = []
    for g in range(len(DIL_GROUPS)):
        cs = slice(g * A_OUT_WIDTH, (g + 1) * A_OUT_WIDTH)
        groups += [qa[:, cs] * QK_SCALE, ka[:, cs], va[:, cs]]
    return jnp.concatenate(groups + [qb * QK_SCALE, rest], axis=1).astype(BF16)


def _rms(x, g):
    return x * lax.rsqrt(jnp.mean(x * x, axis=-1, keepdims=True) + NORM_EPS) * g


def _inproj_kernel(x_ref, g_ref, w_ref, *out_refs):
    u = _rms(x_ref[...], g_ref[...]).astype(BF16)
    off = 0
    for o_ref, width in zip(out_refs, IN_SPLITS):
        o_ref[...] = jnp.dot(u, w_ref[:, off:off + width], preferred_element_type=F32).astype(o_ref.dtype)
        off += width


def _inproj(x2, g, w_bf16, tm):
    n_tok = x2.shape[0]
    return pl.pallas_call(
        _inproj_kernel,
        grid=(n_tok // tm,),
        in_specs=[pl.BlockSpec((tm, D_MODEL), lambda i: (i, 0)),
                  _resident((1, D_MODEL)),
                  _resident(w_bf16.shape)],
        out_specs=[pl.BlockSpec((tm, w), lambda i: (i, 0)) for w in IN_SPLITS],
        out_shape=[jax.ShapeDtypeStruct((n_tok, w), BF16) for w in IN_SPLITS],
        compiler_params=_params(("parallel",)),
        name="inproj",
    )(x2, g, w_bf16)


def _t5_bucket(rel):
    n = -rel
    nb = T5_BUCKETS // 2
    ret = (n < 0).astype(np.int32) * nb
    n = np.abs(n)
    max_exact = nb // 2
    large = max_exact + (np.log(np.maximum(n, 1) / max_exact) / math.log(T5_MAX_DIST / max_exact)
                         * (nb - max_exact)).astype(np.int32)
    large = np.minimum(large, nb - 1)
    return (ret + np.where(n < max_exact, n, large)).astype(np.int32)


def _dilated_bias(t5_table, group, dilation, tq):
    rel = np.arange(tq + 2 * RAD)[None, :] - RAD - np.arange(tq)[:, None]
    tab = t5_table[:, group * A_HEADS_PER_GROUP:(group + 1) * A_HEADS_PER_GROUP].astype(F32)
    in_band = np.abs(rel) <= RAD
    bucket = np.where(in_band, _t5_bucket(rel * dilation), -1)
    bias = jnp.full((A_HEADS_PER_GROUP,) + rel.shape, NEG_INF, F32)
    for b in np.unique(bucket[in_band]):
        bias = jnp.where(jnp.asarray(bucket == b)[None], tab[int(b)][:, None, None], bias)
    return bias


def _dilated_kernel(q_ref, kp_ref, kc_ref, kn_ref, vp_ref, vc_ref, vn_ref, bias_ref, o_ref, l_ref, *, tq, n_tiles):
    i = pl.program_id(2)
    q = q_ref[0]
    k_all = jnp.concatenate([kp_ref[0], kc_ref[0], kn_ref[0]], axis=0)
    v_all = jnp.concatenate([vp_ref[0], vc_ref[0], vn_ref[0]], axis=0)
    nk = tq + 2 * RAD
    col = lax.broadcasted_iota(jnp.int32, (tq, nk), 1)
    first_valid = jnp.where(i == 0, RAD, 0)
    end_valid = jnp.where(i == n_tiles - 1, tq + RAD, nk)
    valid = (col >= first_valid) & (col < end_valid)
    lane = lax.broadcasted_iota(jnp.int32, (tq, A_OUT_WIDTH), 1)
    o_acc = jnp.zeros((tq, A_OUT_WIDTH), F32)
    l_acc = jnp.zeros((tq, A_OUT_WIDTH), F32)
    for h in range(A_HEADS_PER_GROUP):
        head = (lane >= h * HEAD_DIM) & (lane < (h + 1) * HEAD_DIM)
        qh = jnp.where(head, q, jnp.zeros_like(q))
        s = lax.dot_general(qh, k_all, (((1,), (1,)), ((), ())), preferred_element_type=F32)
        s = jnp.where(valid, s + bias_ref[h], NEG_INF)
        m = jnp.max(s, axis=-1, keepdims=True)
        p = jnp.exp(s - m)
        den = jnp.sum(p, axis=-1, keepdims=True)
        o = jnp.dot(p.astype(BF16), v_all, preferred_element_type=F32) / den
        lse = m + jnp.log(den)
        o_acc = jnp.where(head, o, o_acc)
        l_acc = jnp.where(head, lse, l_acc)
    o_ref[0] = o_acc
    l_ref[0] = l_acc


def _dilated_group(qkv, bias, group, dilation, tq):
    bn, s_len, _ = qkv.shape
    sub_len = s_len // dilation
    n_tiles = sub_len // tq
    blk = tq // RAD
    n_rad = sub_len // RAD
    view = qkv.reshape(bn, sub_len, dilation * A_WIDTH)

    def specs(part):
        cur = pl.BlockSpec((1, tq, A_OUT_WIDTH), lambda b, r, i: (b, i, r * 3 + part))
        prev = pl.BlockSpec((1, RAD, A_OUT_WIDTH), lambda b, r, i: (b, jnp.maximum(i * blk - 1, 0), r * 3 + part))
        nxt = pl.BlockSpec((1, RAD, A_OUT_WIDTH),
                           lambda b, r, i: (b, jnp.minimum((i + 1) * blk, n_rad - 1), r * 3 + part))
        return cur, prev, nxt

    out = pl.BlockSpec((1, tq, A_OUT_WIDTH), lambda b, r, i: (b, i, r))
    (q_cur, _, _), (k_cur, k_prev, k_next), (v_cur, v_prev, v_next) = specs(0), specs(1), specs(2)
    o, l = pl.pallas_call(
        functools.partial(_dilated_kernel, tq=tq, n_tiles=n_tiles),
        grid=(bn, dilation, n_tiles),
        in_specs=[q_cur, k_prev, k_cur, k_next, v_prev, v_cur, v_next, _resident(bias.shape)],
        out_specs=[out, out],
        out_shape=[jax.ShapeDtypeStruct((bn, sub_len, dilation * A_OUT_WIDTH), F32)] * 2,
        compiler_params=_params(("parallel", "parallel", "parallel")),
        name=f"dilated{group}",
    )(view, view, view, view, view, view, view, bias)
    return o.reshape(bn, s_len, A_OUT_WIDTH), l.reshape(bn, s_len, A_OUT_WIDTH)


NA_TILE_ROWS = 8
NA_TILE = NA_TILE_ROWS * GRID_W
NA_KEYS = NB_ROWS * GRID_W


def _na_bias(rpb):
    cols = np.arange(GRID_W)
    col_start = np.clip(cols - NB_COLS // 2, 0, GRID_W - NB_COLS)
    col_valid = (cols[None, :] >= col_start[:, None]) & (cols[None, :] < col_start[:, None] + NB_COLS)
    dc = np.clip(cols[None, :] - cols[:, None] + NB_COLS - 1, 0, 2 * NB_COLS - 2)
    rpb = rpb.astype(F32)
    variants = []
    for v in range(NB_ROWS):
        dr = np.clip(np.arange(NB_ROWS) + v, 0, 2 * NB_ROWS - 2)
        tab = rpb[:, dr][:, :, dc]
        tab = jnp.where(jnp.asarray(col_valid)[None, None], tab, NEG_INF)
        variants.append(jnp.transpose(tab, (0, 2, 1, 3)).reshape(B_HEADS // 2, 2 * GRID_W, NA_KEYS))
    return jnp.stack(variants, axis=0)


def _na_kernel(q_ref, kp_ref, kc_ref, kn_ref, vp_ref, vc_ref, vn_ref, bias_ref, o_ref, kwin, vwin, *, rows):
    i = pl.program_id(1)
    kwin[0:NA_TILE] = kp_ref[0]
    kwin[NA_TILE:2 * NA_TILE] = kc_ref[0]
    kwin[2 * NA_TILE:3 * NA_TILE] = kn_ref[0]
    vwin[0:NA_TILE] = vp_ref[0]
    vwin[NA_TILE:2 * NA_TILE] = vc_ref[0]
    vwin[2 * NA_TILE:3 * NA_TILE] = vn_ref[0]
    pair = 2 * HEAD_DIM
    row_id = lax.broadcasted_iota(jnp.int32, (2 * GRID_W, pair), 0)
    lane_id = lax.broadcasted_iota(jnp.int32, (2 * GRID_W, pair), 1)
    own = (row_id < GRID_W) == (lane_id < HEAD_DIM)
    low = lax.broadcasted_iota(jnp.int32, (GRID_W, pair), 1) < HEAD_DIM

    def query_row(a, carry):
        r = i * NA_TILE_ROWS + a
        rs = jnp.clip(r - NB_ROWS // 2, 0, rows - NB_ROWS)
        start = pl.multiple_of((rs - (i - 1) * NA_TILE_ROWS) * GRID_W, GRID_W)
        variant = rs - r + NB_ROWS - 1
        q_rows = pl.ds(pl.multiple_of(a * GRID_W, GRID_W), GRID_W)
        scores = []
        for hp in range(B_HEADS // 2):
            cs = slice(hp * pair, (hp + 1) * pair)
            q2 = q_ref[0, q_rows, cs]
            qs = jnp.where(own, jnp.concatenate([q2, q2], axis=0), jnp.zeros((2 * GRID_W, pair), q2.dtype))
            s = lax.dot_general(qs, kwin[pl.ds(start, NA_KEYS), cs], (((1,), (1,)), ((), ())),
                                preferred_element_type=F32)
            scores.append(s + bias_ref[variant, hp])
        probs = []
        for s in scores:
            p = jnp.exp(s - jnp.max(s, axis=-1, keepdims=True))
            probs.append((p, jnp.sum(p, axis=-1, keepdims=True)))
        outs = []
        for hp, (p, den) in enumerate(probs):
            cs = slice(hp * pair, (hp + 1) * pair)
            o = jnp.dot(p.astype(BF16), vwin[pl.ds(start, NA_KEYS), cs], preferred_element_type=F32) / den
            outs.append(jnp.where(low, o[0:GRID_W], o[GRID_W:]))
        o_ref[0, q_rows, :] = jnp.concatenate(outs, axis=1).astype(o_ref.dtype)
        return carry

    lax.fori_loop(0, NA_TILE_ROWS, query_row, 0)


def _neighbourhood(qb, kb, vb, bias):
    bn, s_len, _ = qb.shape
    rows = s_len // GRID_W
    n_tiles = rows // NA_TILE_ROWS
    cur = pl.BlockSpec((1, NA_TILE, B_WIDTH), lambda b, i: (b, i, 0))
    prev = pl.BlockSpec((1, NA_TILE, B_WIDTH), lambda b, i: (b, jnp.maximum(i - 1, 0), 0))
    nxt = pl.BlockSpec((1, NA_TILE, B_WIDTH), lambda b, i: (b, jnp.minimum(i + 1, n_tiles - 1), 0))
    return pl.pallas_call(
        functools.partial(_na_kernel, rows=rows),
        grid=(bn, n_tiles),
        in_specs=[cur, prev, cur, nxt, prev, cur, nxt, _resident(bias.shape)],
        out_specs=cur,
        out_shape=jax.ShapeDtypeStruct((bn, s_len, B_WIDTH), BF16),
        scratch_shapes=[pltpu.VMEM((3 * NA_TILE, B_WIDTH), BF16)] * 2,
        compiler_params=_params(("parallel", "parallel")),
        name="neighbourhood",
    )(qb, kb, kb, kb, vb, vb, vb, bias)


def _merge_kernel(x_ref, o1_ref, l1_ref, o2_ref, l2_ref, o3_ref, l3_ref, ob_ref, ga_ref, gb_ref,
                  wpa_ref, wpb_ref, wout_ref, gffn_ref, wq_ref, h_ref, u_ref, qp_ref):
    l1, l2, l3 = l1_ref[...], l2_ref[...], l3_ref[...]
    m = jnp.maximum(jnp.maximum(l1, l2), l3)
    w1, w2, w3 = jnp.exp(l1 - m), jnp.exp(l2 - m), jnp.exp(l3 - m)
    oa = (w1 * o1_ref[...] + w2 * o2_ref[...] + w3 * o3_ref[...]) / (w1 + w2 + w3)
    pa = jnp.dot(oa.astype(BF16), wpa_ref[...], preferred_element_type=F32)
    pb = jnp.dot(ob_ref[...], wpb_ref[...], preferred_element_type=F32)
    merged = jax.nn.sigmoid(ga_ref[...].astype(F32)) * pa + jax.nn.sigmoid(gb_ref[...].astype(F32)) * pb
    h = x_ref[...] + jnp.dot(merged.astype(BF16), wout_ref[...], preferred_element_type=F32)
    u = _rms(h, gffn_ref[...])
    for s in range(SUBLANES):
        h_ref[:, s, :] = h[:, s * LANES:(s + 1) * LANES]
        u_ref[:, s, :] = u[:, s * LANES:(s + 1) * LANES]
    qp_ref[...] = jnp.dot(u.astype(BF16), wq_ref[...], preferred_element_type=F32).astype(qp_ref.dtype)


def _merge(x2, dil_outs, ob, ga, gb, wpa, wpb, wout, gffn, wq, tm):
    n_tok = x2.shape[0]
    tok = lambda w: pl.BlockSpec((tm, w), lambda i: (i, 0))
    ins = [x2]
    specs = [tok(D_MODEL)]
    for o, l in dil_outs:
        ins += [o, l]
        specs += [tok(A_OUT_WIDTH), tok(A_OUT_WIDTH)]
    ins += [ob, ga, gb, wpa, wpb, wout, gffn, wq]
    specs += [tok(B_WIDTH), tok(D_MODEL), tok(D_MODEL), _resident(wpa.shape), _resident(wpb.shape),
              _resident(wout.shape), _resident(gffn.shape), _resident(wq.shape)]
    qw = PEER_HEADS * PEER_QDIM
    tile = pl.BlockSpec((tm, SUBLANES, LANES), lambda i: (i, 0, 0))
    return pl.pallas_call(
        _merge_kernel,
        grid=(n_tok // tm,),
        in_specs=specs,
        out_specs=[tile, tile, tok(qw)],
        out_shape=[jax.ShapeDtypeStruct((n_tok, SUBLANES, LANES), F32)] * 2 + [jax.ShapeDtypeStruct((n_tok, qw), BF16)],
        compiler_params=_params(("parallel",)),
        name="merge",
    )(*ins)


def _take_top(problems, count):
    state = [s for s, _, _, _ in problems]
    for j in range(count):
        for i, (_, val_ref, pick_ref, payload) in enumerate(problems):
            s = state[i]
            iota = lax.broadcasted_iota(jnp.int32, s.shape, 0).astype(F32)
            m = jnp.max(s, axis=0, keepdims=True)
            pos = jnp.min(jnp.where(s == m, iota, float(s.shape[0])), axis=0, keepdims=True)
            hit = iota == pos
            val_ref[j:j + 1, :] = m
            pick_ref[j:j + 1, :] = (pos if payload is None
                                    else jnp.max(jnp.where(hit, payload, -1.0), axis=0, keepdims=True))
            state[i] = jnp.where(hit, -jnp.inf, s)


PAIR_COUNTS = tuple(PEER_TOPK // (a + 1) for a in range(PEER_TOPK))
N_PAIRS = sum(PAIR_COUNTS)
PAIR_ROWS = -(-N_PAIRS // SUBLANES) * SUBLANES


TOPK_HEADS_PER_STEP = 8


def _topk_kernel(qp_ref, sk_ref, row_tm_ref, sh_tm_ref, sh_ref, gate_ref,
                 row_scr, sh_scr, va_ref, ia_ref, vb_ref, ib_ref, cand_ref, cidx_ref, best_ref, pick_ref):
    tm = qp_ref.shape[0]
    step = pl.program_id(1)
    stage1 = []
    for hh in range(TOPK_HEADS_PER_STEP):
        for p, (val_ref, idx_ref) in enumerate(((va_ref, ia_ref), (vb_ref, ib_ref))):
            c0 = hh * PEER_QDIM + p * PEER_HALF
            q = qp_ref[:, c0:c0 + PEER_HALF]
            s = lax.dot_general(sk_ref[p], q, (((1,), (1,)), ((), ())), preferred_element_type=F32)
            stage1.append((s, val_ref.at[hh], idx_ref.at[hh], None))
    _take_top(stage1, PEER_TOPK)
    stage2 = []
    for hh in range(TOPK_HEADS_PER_STEP):
        cand_ref[hh, N_PAIRS:, :] = jnp.full((PAIR_ROWS - N_PAIRS, tm), -jnp.inf, F32)
        cidx_ref[hh, N_PAIRS:, :] = jnp.zeros((PAIR_ROWS - N_PAIRS, tm), F32)
        at = 0
        for a, count in enumerate(PAIR_COUNTS):
            cand_ref[hh, at:at + count, :] = va_ref[hh, a:a + 1, :] + vb_ref[hh, 0:count, :]
            cidx_ref[hh, at:at + count, :] = ia_ref[hh, a:a + 1, :] * PEER_NKEYS + ib_ref[hh, 0:count, :]
            at += count
        stage2.append((cand_ref[hh], best_ref.at[hh], pick_ref.at[hh], cidx_ref[hh]))
    _take_top(stage2, PEER_TOPK)
    rows_per_step = TOPK_HEADS_PER_STEP * PEER_TOPK
    best = best_ref[...]
    e = jnp.exp(best - best[:, 0:1])
    gate_ref[0] = (e / jnp.sum(e, axis=1, keepdims=True)).reshape(rows_per_step, tm)
    experts = pick_ref[...].astype(jnp.int32).reshape(rows_per_step, tm)
    shift = jnp.where(experts < HALF_EXPERTS, 16, 0)
    sh_ref[0] = shift
    slots = pl.ds(pl.multiple_of(step * rows_per_step, rows_per_step), rows_per_step)
    row_scr[slots, :] = (experts & (HALF_EXPERTS - 1)) * SUBLANES
    sh_scr[slots, :] = shift

    @pl.when(step == PEER_HEADS // TOPK_HEADS_PER_STEP - 1)
    def _():
        row_tm_ref[0] = row_scr[...].T
        sh_tm_ref[0] = sh_scr[...].T


def _topk(qp, sk_bf16, tm):
    n_tok = qp.shape[0]
    n_tiles = n_tok // tm
    hps = TOPK_HEADS_PER_STEP
    per_step = pl.BlockSpec((1, hps * PEER_TOPK, tm), lambda i, h: (i, h, 0))
    per_tile = pl.BlockSpec((1, tm, PEER_SEL), lambda i, h: (i, 0, 0))
    slot_major = (n_tiles, PEER_SEL, tm)
    token_major = (n_tiles, tm, PEER_SEL)
    return pl.pallas_call(
        _topk_kernel,
        grid=(n_tiles, PEER_HEADS // hps),
        in_specs=[pl.BlockSpec((tm, hps * PEER_QDIM), lambda i, h: (i, h)), _resident(sk_bf16.shape)],
        out_specs=[per_tile, per_tile, per_step, per_step],
        out_shape=[jax.ShapeDtypeStruct(token_major, jnp.int32), jax.ShapeDtypeStruct(token_major, jnp.int32),
                   jax.ShapeDtypeStruct(slot_major, jnp.int32), jax.ShapeDtypeStruct(slot_major, F32)],
        scratch_shapes=[pltpu.VMEM((PEER_SEL, tm), jnp.int32)] * 2
        + [pltpu.VMEM((hps, PEER_TOPK, tm), F32)] * 4 + [pltpu.VMEM((hps, PAIR_ROWS, tm), F32)] * 2
        + [pltpu.VMEM((hps, PEER_TOPK, tm), F32)] * 2,
        compiler_params=_params(("parallel", "arbitrary")),
        name="peer_topk",
    )(qp, sk_bf16)


def _pack_experts(tbl):
    b = lax.bitcast_convert_type(tbl.astype(BF16), jnp.uint16).astype(jnp.uint32)
    w = (b[HALF_EXPERTS:] << 16) | b[:HALF_EXPERTS]
    return lax.bitcast_convert_type(w, jnp.int32).reshape(HALF_EXPERTS * SUBLANES, LANES)


def _table_tile(tbl_ref, offset):
    return tbl_ref[pl.ds(pl.multiple_of(offset, SUBLANES), SUBLANES), :]


PACKED_ROWS = 2 * SUBLANES
V_CHUNK = 32
V_CHUNK_COLS = V_CHUNK * PACKED_ROWS
V_COLS = PEER_SEL * PACKED_ROWS
V_TOKENS_PER_STEP = 4


def _expand_matrix():
    col = np.arange(V_COLS)
    src = np.arange(2 * PEER_SEL)
    e = (src[:, None] % PEER_SEL == col[None, :] // PACKED_ROWS) & (src[:, None] // PEER_SEL == col[None, :] % 2)
    return jnp.asarray(e, BF16)


def _chunk_left(lhs, c):
    c0 = c * V_CHUNK_COLS
    half = V_CHUNK_COLS // 2
    return jnp.concatenate([lhs[:, c0:c0 + half], lhs[:, c0 + half:c0 + V_CHUNK_COLS]], axis=0)


def _chunk_weights(tiles):
    return jnp.concatenate([jnp.concatenate(tiles[:V_CHUNK // 2], axis=0),
                            jnp.concatenate(tiles[V_CHUNK // 2:], axis=0)], axis=1)


def _peer_u_kernel(row_sm, x_ref, gate_ref, sh_ref, sh_tm_ref, e_ref, d_ref, tbl_ref, pair_ref, sums_ref, fe_ref):
    tm = x_ref.shape[0]
    lane = lax.broadcasted_iota(jnp.int32, (PEER_SEL, tm), 1)
    low_tm = sh_tm_ref[0] == 16
    flags = jnp.concatenate([low_tm, jnp.logical_not(low_tm)], axis=1)
    fe_ref[...] = jnp.dot(jnp.where(flags, 1.0, 0.0).astype(BF16), e_ref[...], preferred_element_type=F32)
    blockdiag = d_ref[...]
    half_sub = V_CHUNK // 2

    def token(t):
        xb = lax.bitcast_convert_type(x_ref[t].astype(BF16).astype(F32), jnp.int32)
        x2 = pltpu.bitcast(xb | lax.shift_right_logical(xb, 16), BF16)
        lhs = (fe_ref[pl.ds(t, 1), :] * blockdiag).astype(BF16)
        for c in range(PEER_SEL // V_CHUNK):
            tiles = [pltpu.bitcast(_table_tile(tbl_ref, row_sm[0, t, c * V_CHUNK + j]), BF16) * x2
                     for j in range(V_CHUNK)]
            out = jnp.dot(_chunk_left(lhs, c), _chunk_weights(tiles), preferred_element_type=F32)
            at = pl.multiple_of(t * PEER_SEL + c * V_CHUNK, V_CHUNK)
            sums_ref[pl.ds(at, half_sub), :] = out[0:half_sub, 0:LANES]
            sums_ref[pl.ds(at + half_sub, half_sub), :] = out[half_sub:, LANES:]

    def tokens(i, carry):
        for j in range(V_TOKENS_PER_STEP):
            token(i * V_TOKENS_PER_STEP + j)
        return carry

    lax.fori_loop(0, tm // V_TOKENS_PER_STEP, tokens, 0)

    def finish(i, act):
        for j in range(SUBLANES):
            t = i * SUBLANES + j
            rows = sums_ref[pl.ds(pl.multiple_of(t * PEER_SEL, PEER_SEL), PEER_SEL), :]
            dots = jnp.sum(rows, axis=-1, keepdims=True)
            act = jnp.where(lane == t, dots, act)
        return act

    act = lax.fori_loop(0, tm // SUBLANES, finish, jnp.zeros((PEER_SEL, tm), F32))
    coef = gate_ref[0] * jax.nn.gelu(act)
    low = sh_ref[0] == 16
    pair_ref[:, 0:PEER_SEL] = jnp.where(low, coef, 0.0).T
    pair_ref[:, PEER_SEL:] = jnp.where(low, 0.0, coef).T


def _peer_v_kernel(row_sm, c2_ref, h_ref, g_ref, e_ref, d_ref, tbl_ref, y_ref, ce_ref, acc_ref):
    tm = h_ref.shape[0]
    c2 = c2_ref[...]
    c_hi = c2.astype(BF16)
    c_lo = (c2 - c_hi.astype(F32)).astype(BF16)
    ce_ref[0] = jnp.dot(c_hi, e_ref[...], preferred_element_type=F32)
    ce_ref[1] = jnp.dot(c_lo, e_ref[...], preferred_element_type=F32)
    diag = d_ref[...]

    def token(t):
        lhs = jnp.concatenate([ce_ref[0, pl.ds(t, 1), :] * diag, ce_ref[1, pl.ds(t, 1), :] * diag],
                              axis=0).astype(BF16)
        acc = None
        for c in range(PEER_SEL // V_CHUNK):
            tiles = [pltpu.bitcast(_table_tile(tbl_ref, row_sm[0, t, c * V_CHUNK + j]), BF16) for j in range(V_CHUNK)]
            out = jnp.dot(_chunk_left(lhs, c), _chunk_weights(tiles), preferred_element_type=F32)
            acc = out if acc is None else acc + out
        first = acc[0:SUBLANES, 0:LANES] + acc[SUBLANES:2 * SUBLANES, 0:LANES]
        second = acc[2 * SUBLANES:3 * SUBLANES, LANES:] + acc[3 * SUBLANES:, LANES:]
        acc_ref[t] = h_ref[t] + (first + second)

    def tokens(i, carry):
        for j in range(V_TOKENS_PER_STEP):
            token(i * V_TOKENS_PER_STEP + j)
        return carry

    lax.fori_loop(0, tm // V_TOKENS_PER_STEP, tokens, 0)
    h = acc_ref[...]
    ms = jnp.sum(jnp.sum(h * h, axis=2, keepdims=True), axis=1, keepdims=True) * (1.0 / D_MODEL)
    acc_ref[...] = h * lax.rsqrt(ms + NORM_EPS) * g_ref[...]
    for s in range(SUBLANES):
        y_ref[:, s * LANES:(s + 1) * LANES] = acc_ref[:, s, :]


def _smem_tile(tm):
    return pl.BlockSpec((1, tm, PEER_SEL), lambda i: (i, 0, 0), memory_space=pltpu.SMEM)


def _peer_u(rows_tm, shifts_tm, gates, shifts, u3, tbl, tm):
    n_tiles = gates.shape[0]
    vm = pl.BlockSpec((1, PEER_SEL, tm), lambda i: (i, 0, 0))
    expand = _expand_matrix()
    col = np.arange(V_COLS)
    blockdiag = jnp.asarray((col[None, :] // PACKED_ROWS) % (V_CHUNK // 2) == np.arange(V_CHUNK // 2)[:, None], F32)
    return pl.pallas_call(
        _peer_u_kernel,
        grid=(n_tiles,),
        in_specs=[_smem_tile(tm), pl.BlockSpec((tm, SUBLANES, LANES), lambda i: (i, 0, 0)), vm, vm,
                  pl.BlockSpec((1, tm, PEER_SEL), lambda i: (i, 0, 0)),
                  _resident(expand.shape), _resident(blockdiag.shape), _resident(tbl.shape)],
        out_specs=pl.BlockSpec((tm, 2 * PEER_SEL), lambda i: (i, 0)),
        out_shape=jax.ShapeDtypeStruct((n_tiles * tm, 2 * PEER_SEL), F32),
        scratch_shapes=[pltpu.VMEM((tm * PEER_SEL, LANES), F32), pltpu.VMEM((tm, V_COLS), F32)],
        compiler_params=_params(("arbitrary",)),
        name="peer_u",
    )(rows_tm, u3, gates, shifts, shifts_tm, expand, blockdiag, tbl)


def _peer_v(rows, coef_pairs, h3, g_final, tbl, tm):
    n_tiles = h3.shape[0] // tm
    tok = pl.BlockSpec((tm, SUBLANES, LANES), lambda i: (i, 0, 0))
    expand = _expand_matrix()
    diag = jnp.asarray((np.arange(V_COLS)[None, :] % PACKED_ROWS) // 2 == np.arange(SUBLANES)[:, None], F32)
    return pl.pallas_call(
        _peer_v_kernel,
        grid=(n_tiles,),
        in_specs=[_smem_tile(tm), pl.BlockSpec((tm, 2 * PEER_SEL), lambda i: (i, 0)), tok,
                  _resident(g_final.shape), _resident(expand.shape), _resident(diag.shape), _resident(tbl.shape)],
        out_specs=pl.BlockSpec((tm, D_MODEL), lambda i: (i, 0)),
        out_shape=jax.ShapeDtypeStruct((h3.shape[0], D_MODEL), F32),
        scratch_shapes=[pltpu.VMEM((2, tm, V_COLS), F32), pltpu.VMEM((tm, SUBLANES, LANES), F32)],
        compiler_params=_params(("arbitrary",)),
        name="peer_v",
    )(rows, coef_pairs, h3, g_final, expand, diag, tbl)


PROJ_TILE = 512
MERGE_TILE = 256
PEER_TILE = 128


def _trunk(x, p):
    bn, s_len, d = x.shape
    n_tok = bn * s_len
    x2 = x.reshape(n_tok, d)
    *qkv_groups, qb, kb, vb, ga, gb = _inproj(x2, p["norm_mix"], p["w_in"], PROJ_TILE)
    seq = lambda t: t.reshape(bn, s_len, t.shape[-1])
    dil_outs = []
    for group, (window, dilation) in enumerate(DIL_GROUPS):
        assert window // (2 * dilation) == RAD
        sub_len = s_len // dilation
        assert s_len % dilation == 0 and sub_len % RAD == 0
        tq = min(256, sub_len)
        assert sub_len % tq == 0
        bias = _dilated_bias(p["t5_table"], group, dilation, tq)
        o, l = _dilated_group(seq(qkv_groups[group]), bias, group, dilation, tq)
        dil_outs.append((o.reshape(n_tok, A_OUT_WIDTH), l.reshape(n_tok, A_OUT_WIDTH)))
    assert s_len % NA_TILE == 0 and s_len // GRID_W >= NB_ROWS
    ob = _neighbourhood(seq(qb), seq(kb), seq(vb), p["na_bias"]).reshape(n_tok, B_WIDTH)
    h, u, qp = _merge(x2, dil_outs, ob, ga, gb, p["w_proj_a"], p["w_proj_b"], p["w_out"], p["norm_ffn"],
                      p["w_query"], MERGE_TILE)
    rows_tm, shifts_tm, shifts, gates = _topk(qp, p["sub_keys"], PEER_TILE)
    coef_pairs = _peer_u(rows_tm, shifts_tm, gates, shifts, u, p["expert_u"], PEER_TILE)
    y = _peer_v(rows_tm, coef_pairs, h, p["norm_final"], p["expert_v"], PEER_TILE)
    return y.reshape(bn, s_len, d)


def kernel(x_prompt, x_sample, norm_mix, w_in, w_proj_a, w_proj_b, w_out, rpb, norm_ffn, w_query, sub_keys,
           expert_u, expert_v, t5_table, norm_final):
    assert norm_mix.shape[0] == 1, "single-layer model"
    p = {
        "norm_mix": norm_mix[0].reshape(1, D_MODEL).astype(F32),
        "w_in": _arrange_w_in(w_in[0]),
        "w_proj_a": w_proj_a[0].astype(BF16),
        "w_proj_b": w_proj_b[0].astype(BF16),
        "w_out": w_out[0].astype(BF16),
        "na_bias": _na_bias(rpb[0]),
        "norm_ffn": norm_ffn[0].reshape(1, D_MODEL).astype(F32),
        "w_query": w_query[0].astype(BF16),
        "sub_keys": sub_keys[0].astype(BF16),
        "expert_u": _pack_experts(expert_u[0]),
        "expert_v": _pack_experts(expert_v[0]),
        "t5_table": t5_table,
        "norm_final": norm_final.reshape(SUBLANES, LANES).astype(F32),
    }
    return (_trunk(x_prompt, p), _trunk(x_sample, p))
```

```python
import functools
import math

import numpy as np
import jax
import jax.numpy as jnp
from jax import lax
from jax.experimental import pallas as pl
from jax.experimental.pallas import tpu as pltpu

D_MODEL = 1024
HEAD_DIM = 64
DIL_GROUPS = ((128, 1), (512, 4), (2048, 16))
A_HEADS_PER_GROUP = 4
A_HEADS = A_HEADS_PER_GROUP * len(DIL_GROUPS)
A_WIDTH = A_HEADS * HEAD_DIM
A_OUT_WIDTH = A_HEADS_PER_GROUP * HEAD_DIM
T5_BUCKETS = 32
T5_MAX_DIST = 1024
GRID_W = 64
B_HEADS = 8
B_WIDTH = B_HEADS * HEAD_DIM
NB_ROWS = 8
NB_COLS = 16
PEER_HEADS = 8
PEER_NKEYS = 128
PEER_EXPERTS = PEER_NKEYS * PEER_NKEYS
PEER_QDIM = 256
PEER_HALF = PEER_QDIM // 2
PEER_TOPK = 16
PEER_SEL = PEER_HEADS * PEER_TOPK
NORM_EPS = 1e-6
NEG_INF = -1e30

RAD = 64
SUBLANES = 8
LANES = 128
HALF_EXPERTS = PEER_EXPERTS // 2
VMEM_LIMIT = 48 * 1024 * 1024

F32 = jnp.float32
BF16 = jnp.bfloat16


def _resident(shape):
    nd = len(shape)
    return pl.BlockSpec(shape, lambda *_: (0,) * nd, pipeline_mode=pl.Buffered(1))


def _params(sem):
    return pltpu.CompilerParams(dimension_semantics=sem, vmem_limit_bytes=VMEM_LIMIT)


IN_SPLITS = (A_WIDTH, A_WIDTH, A_WIDTH, B_WIDTH, B_WIDTH, B_WIDTH, D_MODEL, D_MODEL)
QK_SCALE = HEAD_DIM ** -0.5


def _arrange_w_in(w_in):
    qa, ka, va, qb, rest = jnp.split(w_in, [A_WIDTH, 2 * A_WIDTH, 3 * A_WIDTH, 3 * A_WIDTH + B_WIDTH], axis=1)
    groups = []
    for g in range(len(DIL_GROUPS)):
        cs = slice(g * A_OUT_WIDTH, (g + 1) * A_OUT_WIDTH)
        groups += [qa[:, cs] * QK_SCALE, ka[:, cs], va[:, cs]]
    return jnp.concatenate(groups + [qb * QK_SCALE, rest], axis=1).astype(BF16)


def _rms(x, g):
    return x * lax.rsqrt(jnp.mean(x * x, axis=-1, keepdims=True) + NORM_EPS) * g


def _inproj_kernel(x_ref, g_ref, w_ref, *out_refs):
    u = _rms(x_ref[...], g_ref[...]).astype(BF16)
    off = 0
    for o_ref, width in zip(out_refs, IN_SPLITS):
        o_ref[...] = jnp.dot(u, w_ref[:, off:off + width], preferred_element_type=F32).astype(o_ref.dtype)
        off += width


def _inproj(x2, g, w_bf16, tm):
    n_tok = x2.shape[0]
    return pl.pallas_call(
        _inproj_kernel,
        grid=(n_tok // tm,),
        in_specs=[pl.BlockSpec((tm, D_MODEL), lambda i: (i, 0)),
                  _resident((1, D_MODEL)),
                  _resident(w_bf16.shape)],
        out_specs=[pl.BlockSpec((tm, w), lambda i: (i, 0)) for w in IN_SPLITS],
        out_shape=[jax.ShapeDtypeStruct((n_tok, w), BF16) for w in IN_SPLITS],
        compiler_params=_params(("parallel",)),
        name="inproj",
    )(x2, g, w_bf16)


def _t5_bucket(rel):
    n = -rel
    nb = T5_BUCKETS // 2
    ret = (n < 0).astype(np.int32) * nb
    n = np.abs(n)
    max_exact = nb // 2
    large = max_exact + (np.log(np.maximum(n, 1) / max_exact) / math.log(T5_MAX_DIST / max_exact)
                         * (nb - max_exact)).astype(np.int32)
    large = np.minimum(large, nb - 1)
    return (ret + np.where(n < max_exact, n, large)).astype(np.int32)


def _dilated_bias(t5_table, group, dilation, tq):
    rel = np.arange(tq + 2 * RAD)[None, :] - RAD - np.arange(tq)[:, None]
    tab = t5_table[:, group * A_HEADS_PER_GROUP:(group + 1) * A_HEADS_PER_GROUP].astype(F32)
    in_band = np.abs(rel) <= RAD
    bucket = np.where(in_band, _t5_bucket(rel * dilation), -1)
    bias = jnp.full((A_HEADS_PER_GROUP,) + rel.shape, NEG_INF, F32)
    for b in np.unique(bucket[in_band]):
        bias = jnp.where(jnp.asarray(bucket == b)[None], tab[int(b)][:, None, None], bias)
    return bias


def _dilated_kernel(q_ref, kp_ref, kc_ref, kn_ref, vp_ref, vc_ref, vn_ref, bias_ref, o_ref, l_ref, *, tq, n_tiles):
    i = pl.program_id(2)
    q = q_ref[0]
    k_all = jnp.concatenate([kp_ref[0], kc_ref[0], kn_ref[0]], axis=0)
    v_all = jnp.concatenate([vp_ref[0], vc_ref[0], vn_ref[0]], axis=0)
    nk = tq + 2 * RAD
    col = lax.broadcasted_iota(jnp.int32, (tq, nk), 1)
    first_valid = jnp.where(i == 0, RAD, 0)
    end_valid = jnp.where(i == n_tiles - 1, tq + RAD, nk)
    valid = (col >= first_valid) & (col < end_valid)
    lane = lax.broadcasted_iota(jnp.int32, (tq, A_OUT_WIDTH), 1)
    o_acc = jnp.zeros((tq, A_OUT_WIDTH), F32)
    l_acc = jnp.zeros((tq, A_OUT_WIDTH), F32)
    for h in range(A_HEADS_PER_GROUP):
        head = (lane >= h * HEAD_DIM) & (lane < (h + 1) * HEAD_DIM)
        qh = jnp.where(head, q, jnp.zeros_like(q))
        s = lax.dot_general(qh, k_all, (((1,), (1,)), ((), ())), preferred_element_type=F32)
        s = jnp.where(valid, s + bias_ref[h], NEG_INF)
        m = jnp.max(s, axis=-1, keepdims=True)
        p = jnp.exp(s - m)
        den = jnp.sum(p, axis=-1, keepdims=True)
        o = jnp.dot(p.astype(BF16), v_all, preferred_element_type=F32) / den
        lse = m + jnp.log(den)
        o_acc = jnp.where(head, o, o_acc)
        l_acc = jnp.where(head, lse, l_acc)
    o_ref[0] = o_acc
    l_ref[0] = l_acc


def _dilated_group(qkv, bias, group, dilation, tq):
    bn, s_len, _ = qkv.shape
    sub_len = s_len // dilation
    n_tiles = sub_len // tq
    blk = tq // RAD
    n_rad = sub_len // RAD
    view = qkv.reshape(bn, sub_len, dilation * A_WIDTH)

    def specs(part):
        cur = pl.BlockSpec((1, tq, A_OUT_WIDTH), lambda b, r, i: (b, i, r * 3 + part))
        prev = pl.BlockSpec((1, RAD, A_OUT_WIDTH), lambda b, r, i: (b, jnp.maximum(i * blk - 1, 0), r * 3 + part))
        nxt = pl.BlockSpec((1, RAD, A_OUT_WIDTH),
                           lambda b, r, i: (b, jnp.minimum((i + 1) * blk, n_rad - 1), r * 3 + part))
        return cur, prev, nxt

    out = pl.BlockSpec((1, tq, A_OUT_WIDTH), lambda b, r, i: (b, i, r))
    (q_cur, _, _), (k_cur, k_prev, k_next), (v_cur, v_prev, v_next) = specs(0), specs(1), specs(2)
    o, l = pl.pallas_call(
        functools.partial(_dilated_kernel, tq=tq, n_tiles=n_tiles),
        grid=(bn, dilation, n_tiles),
        in_specs=[q_cur, k_prev, k_cur, k_next, v_prev, v_cur, v_next, _resident(bias.shape)],
        out_specs=[out, out],
        out_shape=[jax.ShapeDtypeStruct((bn, sub_len, dilation * A_OUT_WIDTH), F32)] * 2,
        compiler_params=_params(("parallel", "parallel", "parallel")),
        name=f"dilated{group}",
    )(view, view, view, view, view, view, view, bias)
    return o.reshape(bn, s_len, A_OUT_WIDTH), l.reshape(bn, s_len, A_OUT_WIDTH)


NA_TILE_ROWS = 8
NA_TILE = NA_TILE_ROWS * GRID_W
NA_KEYS = NB_ROWS * GRID_W


def _na_bias(rpb):
    cols = np.arange(GRID_W)
    col_start = np.clip(cols - NB_COLS // 2, 0, GRID_W - NB_COLS)
    col_valid = (cols[None, :] >= col_start[:, None]) & (cols[None, :] < col_start[:, None] + NB_COLS)
    dc = np.clip(cols[None, :] - cols[:, None] + NB_COLS - 1, 0, 2 * NB_COLS - 2)
    rpb = rpb.astype(F32)
    variants = []
    for v in range(NB_ROWS):
        dr = np.clip(np.arange(NB_ROWS) + v, 0, 2 * NB_ROWS - 2)
        tab = rpb[:, dr][:, :, dc]
        tab = jnp.where(jnp.asarray(col_valid)[None, None], tab, NEG_INF)
        variants.append(jnp.transpose(tab, (0, 2, 1, 3)).reshape(B_HEADS // 2, 2 * GRID_W, NA_KEYS))
    return jnp.stack(variants, axis=0)


def _na_kernel(q_ref, kp_ref, kc_ref, kn_ref, vp_ref, vc_ref, vn_ref, bias_ref, o_ref, kwin, vwin, *, rows):
    i = pl.program_id(1)
    kwin[0:NA_TILE] = kp_ref[0]
    kwin[NA_TILE:2 * NA_TILE] = kc_ref[0]
    kwin[2 * NA_TILE:3 * NA_TILE] = kn_ref[0]
    vwin[0:NA_TILE] = vp_ref[0]
    vwin[NA_TILE:2 * NA_TILE] = vc_ref[0]
    vwin[2 * NA_TILE:3 * NA_TILE] = vn_ref[0]
    pair = 2 * HEAD_DIM
    row_id = lax.broadcasted_iota(jnp.int32, (2 * GRID_W, pair), 0)
    lane_id = lax.broadcasted_iota(jnp.int32, (2 * GRID_W, pair), 1)
    own = (row_id < GRID_W) == (lane_id < HEAD_DIM)
    low = lax.broadcasted_iota(jnp.int32, (GRID_W, pair), 1) < HEAD_DIM

    def query_row(a, carry):
        r = i * NA_TILE_ROWS + a
        rs = jnp.clip(r - NB_ROWS // 2, 0, rows - NB_ROWS)
        start = pl.multiple_of((rs - (i - 1) * NA_TILE_ROWS) * GRID_W, GRID_W)
        variant = rs - r + NB_ROWS - 1
        q_rows = pl.ds(pl.multiple_of(a * GRID_W, GRID_W), GRID_W)
        scores = []
        for hp in range(B_HEADS // 2):
            cs = slice(hp * pair, (hp + 1) * pair)
            q2 = q_ref[0, q_rows, cs]
            qs = jnp.where(own, jnp.concatenate([q2, q2], axis=0), jnp.zeros((2 * GRID_W, pair), q2.dtype))
            s = lax.dot_general(qs, kwin[pl.ds(start, NA_KEYS), cs], (((1,), (1,)), ((), ())),
                                preferred_element_type=F32)
            scores.append(s + bias_ref[variant, hp])
        probs = []
        for s in scores:
            p = jnp.exp(s - jnp.max(s, axis=-1, keepdims=True))
            probs.append((p, jnp.sum(p, axis=-1, keepdims=True)))
        outs = []
        for hp, (p, den) in enumerate(probs):
            cs = slice(hp * pair, (hp + 1) * pair)
            o = jnp.dot(p.astype(BF16), vwin[pl.ds(start, NA_KEYS), cs], preferred_element_type=F32) / den
            outs.append(jnp.where(low, o[0:GRID_W], o[GRID_W:]))
        o_ref[0, q_rows, :] = jnp.concatenate(outs, axis=1).astype(o_ref.dtype)
        return carry

    lax.fori_loop(0, NA_TILE_ROWS, query_row, 0)


def _neighbourhood(qb, kb, vb, bias):
    bn, s_len, _ = qb.shape
    rows = s_len // GRID_W
    n_tiles = rows // NA_TILE_ROWS
    cur = pl.BlockSpec((1, NA_TILE, B_WIDTH), lambda b, i: (b, i, 0))
    prev = pl.BlockSpec((1, NA_TILE, B_WIDTH), lambda b, i: (b, jnp.maximum(i - 1, 0), 0))
    nxt = pl.BlockSpec((1, NA_TILE, B_WIDTH), lambda b, i: (b, jnp.minimum(i + 1, n_tiles - 1), 0))
    return pl.pallas_call(
        functools.partial(_na_kernel, rows=rows),
        grid=(bn, n_tiles),
        in_specs=[cur, prev, cur, nxt, prev, cur, nxt, _resident(bias.shape)],
        out_specs=cur,
        out_shape=jax.ShapeDtypeStruct((bn, s_len, B_WIDTH), BF16),
        scratch_shapes=[pltpu.VMEM((3 * NA_TILE, B_WIDTH), BF16)] * 2,
        compiler_params=_params(("parallel", "parallel")),
        name="neighbourhood",
    )(qb, kb, kb, kb, vb, vb, vb, bias)


def _merge_kernel(x_ref, o1_ref, l1_ref, o2_ref, l2_ref, o3_ref, l3_ref, ob_ref, ga_ref, gb_ref,
                  wpa_ref, wpb_ref, wout_ref, gffn_ref, wq_ref, h_ref, u_ref, qp_ref):
    l1, l2, l3 = l1_ref[...], l2_ref[...], l3_ref[...]
    m = jnp.maximum(jnp.maximum(l1, l2), l3)
    w1, w2, w3 = jnp.exp(l1 - m), jnp.exp(l2 - m), jnp.exp(l3 - m)
    oa = (w1 * o1_ref[...] + w2 * o2_ref[...] + w3 * o3_ref[...]) / (w1 + w2 + w3)
    pa = jnp.dot(oa.astype(BF16), wpa_ref[...], preferred_element_type=F32)
    pb = jnp.dot(ob_ref[...], wpb_ref[...], preferred_element_type=F32)
    merged = jax.nn.sigmoid(ga_ref[...].astype(F32)) * pa + jax.nn.sigmoid(gb_ref[...].astype(F32)) * pb
    h = x_ref[...] + jnp.dot(merged.astype(BF16), wout_ref[...], preferred_element_type=F32)
    u = _rms(h, gffn_ref[...])
    for s in range(SUBLANES):
        h_ref[:, s, :] = h[:, s * LANES:(s + 1) * LANES]
        u_ref[:, s, :] = u[:, s * LANES:(s + 1) * LANES]
    qp_ref[...] = jnp.dot(u.astype(BF16), wq_ref[...], preferred_element_type=F32).astype(qp_ref.dtype)


def _merge(x2, dil_outs, ob, ga, gb, wpa, wpb, wout, gffn, wq, tm):
    n_tok = x2.shape[0]
    tok = lambda w: pl.BlockSpec((tm, w), lambda i: (i, 0))
    ins = [x2]
    specs = [tok(D_MODEL)]
    for o, l in dil_outs:
        ins += [o, l]
        specs += [tok(A_OUT_WIDTH), tok(A_OUT_WIDTH)]
    ins += [ob, ga, gb, wpa, wpb, wout, gffn, wq]
    specs += [tok(B_WIDTH), tok(D_MODEL), tok(D_MODEL), _resident(wpa.shape), _resident(wpb.shape),
              _resident(wout.shape), _resident(gffn.shape), _resident(wq.shape)]
    qw = PEER_HEADS * PEER_QDIM
    tile = pl.BlockSpec((tm, SUBLANES, LANES), lambda i: (i, 0, 0))
    return pl.pallas_call(
        _merge_kernel,
        grid=(n_tok // tm,),
        in_specs=specs,
        out_specs=[tile, tile, tok(qw)],
        out_shape=[jax.ShapeDtypeStruct((n_tok, SUBLANES, LANES), F32)] * 2 + [jax.ShapeDtypeStruct((n_tok, qw), BF16)],
        compiler_params=_params(("parallel",)),
        name="merge",
    )(*ins)


def _take_top(problems, count):
    state = [s for s, _, _, _ in problems]
    for j in range(count):
        for i, (_, val_ref, pick_ref, payload) in enumerate(problems):
            s = state[i]
            iota = lax.broadcasted_iota(jnp.int32, s.shape, 0).astype(F32)
            m = jnp.max(s, axis=0, keepdims=True)
            pos = jnp.min(jnp.where(s == m, iota, float(s.shape[0])), axis=0, keepdims=True)
            hit = iota == pos
            val_ref[j:j + 1, :] = m
            pick_ref[j:j + 1, :] = (pos if payload is None
                                    else jnp.max(jnp.where(hit, payload, -1.0), axis=0, keepdims=True))
            state[i] = jnp.where(hit, -jnp.inf, s)


PAIR_COUNTS = tuple(PEER_TOPK // (a + 1) for a in range(PEER_TOPK))
N_PAIRS = sum(PAIR_COUNTS)
PAIR_ROWS = -(-N_PAIRS // SUBLANES) * SUBLANES


TOPK_HEADS_PER_STEP = 8


def _topk_kernel(qp_ref, sk_ref, row_tm_ref, sh_tm_ref, sh_ref, gate_ref,
                 row_scr, sh_scr, va_ref, ia_ref, vb_ref, ib_ref, cand_ref, cidx_ref, best_ref, pick_ref):
    tm = qp_ref.shape[0]
    step = pl.program_id(1)
    stage1 = []
    for hh in range(TOPK_HEADS_PER_STEP):
        for p, (val_ref, idx_ref) in enumerate(((va_ref, ia_ref), (vb_ref, ib_ref))):
            c0 = hh * PEER_QDIM + p * PEER_HALF
            q = qp_ref[:, c0:c0 + PEER_HALF]
            s = lax.dot_general(sk_ref[p], q, (((1,), (1,)), ((), ())), preferred_element_type=F32)
            stage1.append((s, val_ref.at[hh], idx_ref.at[hh], None))
    _take_top(stage1, PEER_TOPK)
    stage2 = []
    for hh in range(TOPK_HEADS_PER_STEP):
        cand_ref[hh, N_PAIRS:, :] = jnp.full((PAIR_ROWS - N_PAIRS, tm), -jnp.inf, F32)
        cidx_ref[hh, N_PAIRS:, :] = jnp.zeros((PAIR_ROWS - N_PAIRS, tm), F32)
        at = 0
        for a, count in enumerate(PAIR_COUNTS):
            cand_ref[hh, at:at + count, :] = va_ref[hh, a:a + 1, :] + vb_ref[hh, 0:count, :]
            cidx_ref[hh, at:at + count, :] = ia_ref[hh, a:a + 1, :] * PEER_NKEYS + ib_ref[hh, 0:count, :]
            at += count
        stage2.append((cand_ref[hh], best_ref.at[hh], pick_ref.at[hh], cidx_ref[hh]))
    _take_top(stage2, PEER_TOPK)
    rows_per_step = TOPK_HEADS_PER_STEP * PEER_TOPK
    best = best_ref[...]
    e = jnp.exp(best - best[:, 0:1])
    gate_ref[0] = (e / jnp.sum(e, axis=1, keepdims=True)).reshape(rows_per_step, tm)
    experts = pick_ref[...].astype(jnp.int32).reshape(rows_per_step, tm)
    shift = jnp.where(experts < HALF_EXPERTS, 16, 0)
    sh_ref[0] = shift
    slots = pl.ds(pl.multiple_of(step * rows_per_step, rows_per_step), rows_per_step)
    row_scr[slots, :] = (experts & (HALF_EXPERTS - 1)) * SUBLANES
    sh_scr[slots, :] = shift

    @pl.when(step == PEER_HEADS // TOPK_HEADS_PER_STEP - 1)
    def _():
        row_tm_ref[0] = row_scr[...].T
        sh_tm_ref[0] = sh_scr[...].T


def _topk(qp, sk_bf16, tm):
    n_tok = qp.shape[0]
    n_tiles = n_tok // tm
    hps = TOPK_HEADS_PER_STEP
    per_step = pl.BlockSpec((1, hps * PEER_TOPK, tm), lambda i, h: (i, h, 0))
    per_tile = pl.BlockSpec((1, tm, PEER_SEL), lambda i, h: (i, 0, 0))
    slot_major = (n_tiles, PEER_SEL, tm)
    token_major = (n_tiles, tm, PEER_SEL)
    return pl.pallas_call(
        _topk_kernel,
        grid=(n_tiles, PEER_HEADS // hps),
        in_specs=[pl.BlockSpec((tm, hps * PEER_QDIM), lambda i, h: (i, h)), _resident(sk_bf16.shape)],
        out_specs=[per_tile, per_tile, per_step, per_step],
        out_shape=[jax.ShapeDtypeStruct(token_major, jnp.int32), jax.ShapeDtypeStruct(token_major, jnp.int32),
                   jax.ShapeDtypeStruct(slot_major, jnp.int32), jax.ShapeDtypeStruct(slot_major, F32)],
        scratch_shapes=[pltpu.VMEM((PEER_SEL, tm), jnp.int32)] * 2
        + [pltpu.VMEM((hps, PEER_TOPK, tm), F32)] * 4 + [pltpu.VMEM((hps, PAIR_ROWS, tm), F32)] * 2
        + [pltpu.VMEM((hps, PEER_TOPK, tm), F32)] * 2,
        compiler_params=_params(("parallel", "arbitrary")),
        name="peer_topk",
    )(qp, sk_bf16)


def _pack_experts(tbl):
    b = lax.bitcast_convert_type(tbl.astype(BF16), jnp.uint16).astype(jnp.uint32)
    w = (b[HALF_EXPERTS:] << 16) | b[:HALF_EXPERTS]
    return lax.bitcast_convert_type(w, jnp.int32).reshape(HALF_EXPERTS * SUBLANES, LANES)


def _table_tile(tbl_ref, offset):
    return tbl_ref[pl.ds(pl.multiple_of(offset, SUBLANES), SUBLANES), :]


PACKED_ROWS = 2 * SUBLANES
V_CHUNK = 32
V_CHUNK_COLS = V_CHUNK * PACKED_ROWS
V_COLS = PEER_SEL * PACKED_ROWS


def _expand_matrix():
    col = np.arange(V_COLS)
    src = np.arange(2 * PEER_SEL)
    e = (src[:, None] % PEER_SEL == col[None, :] // PACKED_ROWS) & (src[:, None] // PEER_SEL == col[None, :] % 2)
    return jnp.asarray(e, BF16)


def _chunk_left(lhs, c):
    c0 = c * V_CHUNK_COLS
    half = V_CHUNK_COLS // 2
    return jnp.concatenate([lhs[:, c0:c0 + half], lhs[:, c0 + half:c0 + V_CHUNK_COLS]], axis=0)


def _chunk_weights(tiles):
    return jnp.concatenate([jnp.concatenate(tiles[:V_CHUNK // 2], axis=0),
                            jnp.concatenate(tiles[V_CHUNK // 2:], axis=0)], axis=1)


def _peer_u_kernel(rows_hbm, x_ref, gate_ref, sh_ref, sh_tm_ref, e_ref, d_ref, tbl_ref, pair_ref,
                   sums_ref, fe_ref, stage_sm, sems):
    tm = x_ref.shape[0]
    lane = lax.broadcasted_iota(jnp.int32, (PEER_SEL, tm), 1)
    low_tm = sh_tm_ref[0] == 16
    flags = jnp.concatenate([low_tm, jnp.logical_not(low_tm)], axis=1)
    fe_ref[...] = jnp.dot(jnp.where(flags, 1.0, 0.0).astype(BF16), e_ref[...], preferred_element_type=F32)
    blockdiag = d_ref[...]
    half_sub = V_CHUNK // 2

    def token(t, offset):
        xb = lax.bitcast_convert_type(x_ref[t].astype(BF16).astype(F32), jnp.int32)
        x2 = pltpu.bitcast(xb | lax.shift_right_logical(xb, 16), BF16)
        lhs = (fe_ref[pl.ds(t, 1), :] * blockdiag).astype(BF16)
        for c in range(PEER_SEL // V_CHUNK):
            tiles = [pltpu.bitcast(_table_tile(tbl_ref, offset(c * V_CHUNK + j)), BF16) * x2
                     for j in range(V_CHUNK)]
            out = jnp.dot(_chunk_left(lhs, c), _chunk_weights(tiles), preferred_element_type=F32)
            at = pl.multiple_of(t * PEER_SEL + c * V_CHUNK, V_CHUNK)
            sums_ref[pl.ds(at, half_sub), :] = out[0:half_sub, 0:LANES]
            sums_ref[pl.ds(at + half_sub, half_sub), :] = out[half_sub:, LANES:]

    _for_staged_tokens(rows_hbm, stage_sm, sems, tm, token)

    def finish(i, act):
        for j in range(SUBLANES):
            t = i * SUBLANES + j
            rows = sums_ref[pl.ds(pl.multiple_of(t * PEER_SEL, PEER_SEL), PEER_SEL), :]
            dots = jnp.sum(rows, axis=-1, keepdims=True)
            act = jnp.where(lane == t, dots, act)
        return act

    act = lax.fori_loop(0, tm // SUBLANES, finish, jnp.zeros((PEER_SEL, tm), F32))
    coef = gate_ref[0] * jax.nn.gelu(act)
    low = sh_ref[0] == 16
    pair_ref[:, 0:PEER_SEL] = jnp.where(low, coef, 0.0).T
    pair_ref[:, PEER_SEL:] = jnp.where(low, 0.0, coef).T


STAGE_TOKENS = SUBLANES
STAGE_SLOTS = 4


def _for_staged_tokens(rows_hbm, stage_sm, sems, tm, token_fn):
    tile = pl.program_id(0)
    n_blocks = tm // STAGE_TOKENS

    def copy(block, slot):
        rows = pl.ds(pl.multiple_of(block * STAGE_TOKENS, STAGE_TOKENS), STAGE_TOKENS)
        return pltpu.make_async_copy(rows_hbm.at[tile, rows], stage_sm.at[slot], sems.at[slot])

    for slot in range(STAGE_SLOTS):
        copy(slot, slot).start()

    def round_(i, carry):
        for slot in range(STAGE_SLOTS):
            block = i * STAGE_SLOTS + slot
            copy(block, slot).wait()
            for j in range(STAGE_TOKENS):
                token_fn(block * STAGE_TOKENS + j, lambda k, slot=slot, j=j: stage_sm[slot, j, k])
            copy(jnp.minimum(block + STAGE_SLOTS, n_blocks - 1), slot).start()
        return carry

    lax.fori_loop(0, n_blocks // STAGE_SLOTS, round_, 0)
    for slot in range(STAGE_SLOTS):
        copy(n_blocks - 1, slot).wait()


def _peer_v_kernel(rows_hbm, c2_ref, h_ref, g_ref, e_ref, d_ref, tbl_ref, y_ref, ce_ref, acc_ref, stage_sm, sems):
    tm = h_ref.shape[0]
    c2 = c2_ref[...]
    c_hi = c2.astype(BF16)
    c_lo = (c2 - c_hi.astype(F32)).astype(BF16)
    ce_ref[0] = jnp.dot(c_hi, e_ref[...], preferred_element_type=F32)
    ce_ref[1] = jnp.dot(c_lo, e_ref[...], preferred_element_type=F32)
    diag = d_ref[...]

    def token(t, offset):
        lhs = jnp.concatenate([ce_ref[0, pl.ds(t, 1), :] * diag, ce_ref[1, pl.ds(t, 1), :] * diag],
                              axis=0).astype(BF16)
        acc = None
        for c in range(PEER_SEL // V_CHUNK):
            tiles = [pltpu.bitcast(_table_tile(tbl_ref, offset(c * V_CHUNK + j)), BF16) for j in range(V_CHUNK)]
            out = jnp.dot(_chunk_left(lhs, c), _chunk_weights(tiles), preferred_element_type=F32)
            acc = out if acc is None else acc + out
        first = acc[0:SUBLANES, 0:LANES] + acc[SUBLANES:2 * SUBLANES, 0:LANES]
        second = acc[2 * SUBLANES:3 * SUBLANES, LANES:] + acc[3 * SUBLANES:, LANES:]
        acc_ref[t] = h_ref[t] + (first + second)

    _for_staged_tokens(rows_hbm, stage_sm, sems, tm, token)
    h = acc_ref[...]
    ms = jnp.sum(jnp.sum(h * h, axis=2, keepdims=True), axis=1, keepdims=True) * (1.0 / D_MODEL)
    acc_ref[...] = h * lax.rsqrt(ms + NORM_EPS) * g_ref[...]
    for s in range(SUBLANES):
        y_ref[:, s * LANES:(s + 1) * LANES] = acc_ref[:, s, :]


def _peer_u(rows_tm, shifts_tm, gates, shifts, u3, tbl, tm):
    n_tiles = gates.shape[0]
    vm = pl.BlockSpec((1, PEER_SEL, tm), lambda i: (i, 0, 0))
    expand = _expand_matrix()
    col = np.arange(V_COLS)
    blockdiag = jnp.asarray((col[None, :] // PACKED_ROWS) % (V_CHUNK // 2) == np.arange(V_CHUNK // 2)[:, None], F32)
    return pl.pallas_call(
        _peer_u_kernel,
        grid=(n_tiles,),
        in_specs=[pl.BlockSpec(memory_space=pl.ANY), pl.BlockSpec((tm, SUBLANES, LANES), lambda i: (i, 0, 0)), vm, vm,
                  pl.BlockSpec((1, tm, PEER_SEL), lambda i: (i, 0, 0)),
                  _resident(expand.shape), _resident(blockdiag.shape), _resident(tbl.shape)],
        out_specs=pl.BlockSpec((tm, 2 * PEER_SEL), lambda i: (i, 0)),
        out_shape=jax.ShapeDtypeStruct((n_tiles * tm, 2 * PEER_SEL), F32),
        scratch_shapes=[pltpu.VMEM((tm * PEER_SEL, LANES), F32), pltpu.VMEM((tm, V_COLS), F32),
                        pltpu.SMEM((STAGE_SLOTS, STAGE_TOKENS, PEER_SEL), jnp.int32),
                        pltpu.SemaphoreType.DMA((STAGE_SLOTS,))],
        compiler_params=_params(("arbitrary",)),
        name="peer_u",
    )(rows_tm, u3, gates, shifts, shifts_tm, expand, blockdiag, tbl)


def _peer_v(rows, coef_pairs, h3, g_final, tbl, tm):
    n_tiles = h3.shape[0] // tm
    tok = pl.BlockSpec((tm, SUBLANES, LANES), lambda i: (i, 0, 0))
    expand = _expand_matrix()
    diag = jnp.asarray((np.arange(V_COLS)[None, :] % PACKED_ROWS) // 2 == np.arange(SUBLANES)[:, None], F32)
    return pl.pallas_call(
        _peer_v_kernel,
        grid=(n_tiles,),
        in_specs=[pl.BlockSpec(memory_space=pl.ANY), pl.BlockSpec((tm, 2 * PEER_SEL), lambda i: (i, 0)), tok,
                  _resident(g_final.shape), _resident(expand.shape), _resident(diag.shape), _resident(tbl.shape)],
        out_specs=pl.BlockSpec((tm, D_MODEL), lambda i: (i, 0)),
        out_shape=jax.ShapeDtypeStruct((h3.shape[0], D_MODEL), F32),
        scratch_shapes=[pltpu.VMEM((2, tm, V_COLS), F32), pltpu.VMEM((tm, SUBLANES, LANES), F32),
                        pltpu.SMEM((STAGE_SLOTS, STAGE_TOKENS, PEER_SEL), jnp.int32),
                        pltpu.SemaphoreType.DMA((STAGE_SLOTS,))],
        compiler_params=_params(("arbitrary",)),
        name="peer_v",
    )(rows, coef_pairs, h3, g_final, expand, diag, tbl)


PROJ_TILE = 512
MERGE_TILE = 256
PEER_TILE = 128


def _trunk(x, p):
    bn, s_len, d = x.shape
    n_tok = bn * s_len
    x2 = x.reshape(n_tok, d)
    *qkv_groups, qb, kb, vb, ga, gb = _inproj(x2, p["norm_mix"], p["w_in"], PROJ_TILE)
    seq = lambda t: t.reshape(bn, s_len, t.shape[-1])
    dil_outs = []
    for group, (window, dilation) in enumerate(DIL_GROUPS):
        assert window // (2 * dilation) == RAD
        sub_len = s_len // dilation
        assert s_len % dilation == 0 and sub_len % RAD == 0
        tq = min(256, sub_len)
        assert sub_len % tq == 0
        bias = _dilated_bias(p["t5_table"], group, dilation, tq)
        o, l = _dilated_group(seq(qkv_groups[group]), bias, group, dilation, tq)
        dil_outs.append((o.reshape(n_tok, A_OUT_WIDTH), l.reshape(n_tok, A_OUT_WIDTH)))
    assert s_len % NA_TILE == 0 and s_len // GRID_W >= NB_ROWS
    ob = _neighbourhood(seq(qb), seq(kb), seq(vb), p["na_bias"]).reshape(n_tok, B_WIDTH)
    h, u, qp = _merge(x2, dil_outs, ob, ga, gb, p["w_proj_a"], p["w_proj_b"], p["w_out"], p["norm_ffn"],
                      p["w_query"], MERGE_TILE)
    rows_tm, shifts_tm, shifts, gates = _topk(qp, p["sub_keys"], PEER_TILE)
    coef_pairs = _peer_u(rows_tm, shifts_tm, gates, shifts, u, p["expert_u"], PEER_TILE)
    y = _peer_v(rows_tm, coef_pairs, h, p["norm_final"], p["expert_v"], PEER_TILE)
    return y.reshape(bn, s_len, d)


def kernel(x_prompt, x_sample, norm_mix, w_in, w_proj_a, w_proj_b, w_out, rpb, norm_ffn, w_query, sub_keys,
           expert_u, expert_v, t5_table, norm_final):
    assert norm_mix.shape[0] == 1, "single-layer model"
    p = {
        "norm_mix": norm_mix[0].reshape(1, D_MODEL).astype(F32),
        "w_in": _arrange_w_in(w_in[0]),
        "w_proj_a": w_proj_a[0].astype(BF16),
        "w_proj_b": w_proj_b[0].astype(BF16),
        "w_out": w_out[0].astype(BF16),
        "na_bias": _na_bias(rpb[0]),
        "norm_ffn": norm_ffn[0].reshape(1, D_MODEL).astype(F32),
        "w_query": w_query[0].astype(BF16),
        "sub_keys": sub_keys[0].astype(BF16),
        "expert_u": _pack_experts(expert_u[0]),
        "expert_v": _pack_experts(expert_v[0]),
        "t5_table": t5_table,
        "norm_final": norm_final.reshape(SUBLANES, LANES).astype(F32),
    }
    return (_trunk(x_prompt, p), _trunk(x_sample, p))
```

```python
import functools
import math

import numpy as np
import jax
import jax.numpy as jnp
from jax import lax
from jax.experimental import pallas as pl
from jax.experimental.pallas import tpu as pltpu

D_MODEL = 1024
HEAD_DIM = 64
DIL_GROUPS = ((128, 1), (512, 4), (2048, 16))
A_HEADS_PER_GROUP = 4
A_HEADS = A_HEADS_PER_GROUP * len(DIL_GROUPS)
A_WIDTH = A_HEADS * HEAD_DIM
A_OUT_WIDTH = A_HEADS_PER_GROUP * HEAD_DIM
T5_BUCKETS = 32
T5_MAX_DIST = 1024
GRID_W = 64
B_HEADS = 8
B_WIDTH = B_HEADS * HEAD_DIM
NB_ROWS = 8
NB_COLS = 16
PEER_HEADS = 8
PEER_NKEYS = 128
PEER_EXPERTS = PEER_NKEYS * PEER_NKEYS
PEER_QDIM = 256
PEER_HALF = PEER_QDIM // 2
PEER_TOPK = 16
PEER_SEL = PEER_HEADS * PEER_TOPK
NORM_EPS = 1e-6
NEG_INF = -1e30

RAD = 64
SUBLANES = 8
LANES = 128
HALF_EXPERTS = PEER_EXPERTS // 2
VMEM_LIMIT = 48 * 1024 * 1024

F32 = jnp.float32
BF16 = jnp.bfloat16


def _resident(shape):
    nd = len(shape)
    return pl.BlockSpec(shape, lambda *_: (0,) * nd, pipeline_mode=pl.Buffered(1))


def _params(sem):
    return pltpu.CompilerParams(dimension_semantics=sem, vmem_limit_bytes=VMEM_LIMIT)


IN_SPLITS = (A_WIDTH, A_WIDTH, A_WIDTH, B_WIDTH, B_WIDTH, B_WIDTH, D_MODEL, D_MODEL)
QK_SCALE = HEAD_DIM ** -0.5


def _arrange_w_in(w_in):
    qa, ka, va, qb, rest = jnp.split(w_in, [A_WIDTH, 2 * A_WIDTH, 3 * A_WIDTH, 3 * A_WIDTH + B_WIDTH], axis=1)
    groups = []
    for g in range(len(DIL_GROUPS)):
        cs = slice(g * A_OUT_WIDTH, (g + 1) * A_OUT_WIDTH)
        groups += [qa[:, cs] * QK_SCALE, ka[:, cs], va[:, cs]]
    return jnp.concatenate(groups + [qb * QK_SCALE, rest], axis=1).astype(BF16)


def _rms(x, g):
    return x * lax.rsqrt(jnp.mean(x * x, axis=-1, keepdims=True) + NORM_EPS) * g


def _inproj_kernel(x_ref, g_ref, w_ref, *out_refs):
    u = _rms(x_ref[...], g_ref[...]).astype(BF16)
    off = 0
    for o_ref, width in zip(out_refs, IN_SPLITS):
        o_ref[...] = jnp.dot(u, w_ref[:, off:off + width], preferred_element_type=F32).astype(o_ref.dtype)
        off += width


def _inproj(x2, g, w_bf16, tm):
    n_tok = x2.shape[0]
    return pl.pallas_call(
        _inproj_kernel,
        grid=(n_tok // tm,),
        in_specs=[pl.BlockSpec((tm, D_MODEL), lambda i: (i, 0)),
                  _resident((1, D_MODEL)),
                  _resident(w_bf16.shape)],
        out_specs=[pl.BlockSpec((tm, w), lambda i: (i, 0)) for w in IN_SPLITS],
        out_shape=[jax.ShapeDtypeStruct((n_tok, w), BF16) for w in IN_SPLITS],
        compiler_params=_params(("parallel",)),
        name="inproj",
    )(x2, g, w_bf16)


def _t5_bucket(rel):
    n = -rel
    nb = T5_BUCKETS // 2
    ret = (n < 0).astype(np.int32) * nb
    n = np.abs(n)
    max_exact = nb // 2
    large = max_exact + (np.log(np.maximum(n, 1) / max_exact) / math.log(T5_MAX_DIST / max_exact)
                         * (nb - max_exact)).astype(np.int32)
    large = np.minimum(large, nb - 1)
    return (ret + np.where(n < max_exact, n, large)).astype(np.int32)


def _dilated_bias(t5_table, group, dilation, tq):
    rel = np.arange(tq + 2 * RAD)[None, :] - RAD - np.arange(tq)[:, None]
    tab = t5_table[:, group * A_HEADS_PER_GROUP:(group + 1) * A_HEADS_PER_GROUP].astype(F32)
    in_band = np.abs(rel) <= RAD
    bucket = np.where(in_band, _t5_bucket(rel * dilation), -1)
    bias = jnp.full((A_HEADS_PER_GROUP,) + rel.shape, NEG_INF, F32)
    for b in np.unique(bucket[in_band]):
        bias = jnp.where(jnp.asarray(bucket == b)[None], tab[int(b)][:, None, None], bias)
    return bias


def _dilated_kernel(q_ref, kp_ref, kc_ref, kn_ref, vp_ref, vc_ref, vn_ref, bias_ref, o_ref, l_ref, *, tq, n_tiles):
    i = pl.program_id(2)
    q = q_ref[0]
    k_all = jnp.concatenate([kp_ref[0], kc_ref[0], kn_ref[0]], axis=0)
    v_all = jnp.concatenate([vp_ref[0], vc_ref[0], vn_ref[0]], axis=0)
    nk = tq + 2 * RAD
    col = lax.broadcasted_iota(jnp.int32, (tq, nk), 1)
    first_valid = jnp.where(i == 0, RAD, 0)
    end_valid = jnp.where(i == n_tiles - 1, tq + RAD, nk)
    valid = (col >= first_valid) & (col < end_valid)
    lane = lax.broadcasted_iota(jnp.int32, (tq, A_OUT_WIDTH), 1)
    o_acc = jnp.zeros((tq, A_OUT_WIDTH), F32)
    l_acc = jnp.zeros((tq, A_OUT_WIDTH), F32)
    for h in range(A_HEADS_PER_GROUP):
        head = (lane >= h * HEAD_DIM) & (lane < (h + 1) * HEAD_DIM)
        qh = jnp.where(head, q, jnp.zeros_like(q))
        s = lax.dot_general(qh, k_all, (((1,), (1,)), ((), ())), preferred_element_type=F32)
        s = jnp.where(valid, s + bias_ref[h], NEG_INF)
        m = jnp.max(s, axis=-1, keepdims=True)
        p = jnp.exp(s - m)
        den = jnp.sum(p, axis=-1, keepdims=True)
        o = jnp.dot(p.astype(BF16), v_all, preferred_element_type=F32) / den
        lse = m + jnp.log(den)
        o_acc = jnp.where(head, o, o_acc)
        l_acc = jnp.where(head, lse, l_acc)
    o_ref[0] = o_acc
    l_ref[0] = l_acc


def _dilated_group(qkv, bias, group, dilation, tq):
    bn, s_len, _ = qkv.shape
    sub_len = s_len // dilation
    n_tiles = sub_len // tq
    blk = tq // RAD
    n_rad = sub_len // RAD
    view = qkv.reshape(bn, sub_len, dilation * A_WIDTH)

    def specs(part):
        cur = pl.BlockSpec((1, tq, A_OUT_WIDTH), lambda b, r, i: (b, i, r * 3 + part))
        prev = pl.BlockSpec((1, RAD, A_OUT_WIDTH), lambda b, r, i: (b, jnp.maximum(i * blk - 1, 0), r * 3 + part))
        nxt = pl.BlockSpec((1, RAD, A_OUT_WIDTH),
                           lambda b, r, i: (b, jnp.minimum((i + 1) * blk, n_rad - 1), r * 3 + part))
        return cur, prev, nxt

    out = pl.BlockSpec((1, tq, A_OUT_WIDTH), lambda b, r, i: (b, i, r))
    (q_cur, _, _), (k_cur, k_prev, k_next), (v_cur, v_prev, v_next) = specs(0), specs(1), specs(2)
    o, l = pl.pallas_call(
        functools.partial(_dilated_kernel, tq=tq, n_tiles=n_tiles),
        grid=(bn, dilation, n_tiles),
        in_specs=[q_cur, k_prev, k_cur, k_next, v_prev, v_cur, v_next, _resident(bias.shape)],
        out_specs=[out, out],
        out_shape=[jax.ShapeDtypeStruct((bn, sub_len, dilation * A_OUT_WIDTH), F32)] * 2,
        compiler_params=_params(("parallel", "parallel", "parallel")),
        name=f"dilated{group}",
    )(view, view, view, view, view, view, view, bias)
    return o.reshape(bn, s_len, A_OUT_WIDTH), l.reshape(bn, s_len, A_OUT_WIDTH)


NA_TILE_ROWS = 8
NA_TILE = NA_TILE_ROWS * GRID_W
NA_KEYS = NB_ROWS * GRID_W


def _na_bias(rpb):
    cols = np.arange(GRID_W)
    col_start = np.clip(cols - NB_COLS // 2, 0, GRID_W - NB_COLS)
    col_valid = (cols[None, :] >= col_start[:, None]) & (cols[None, :] < col_start[:, None] + NB_COLS)
    dc = np.clip(cols[None, :] - cols[:, None] + NB_COLS - 1, 0, 2 * NB_COLS - 2)
    rpb = rpb.astype(F32)
    variants = []
    for v in range(NB_ROWS):
        dr = np.clip(np.arange(NB_ROWS) + v, 0, 2 * NB_ROWS - 2)
        tab = rpb[:, dr][:, :, dc]
        tab = jnp.where(jnp.asarray(col_valid)[None, None], tab, NEG_INF)
        variants.append(jnp.transpose(tab, (0, 2, 1, 3)).reshape(B_HEADS // 2, 2 * GRID_W, NA_KEYS))
    return jnp.stack(variants, axis=0)


def _na_kernel(q_ref, kp_ref, kc_ref, kn_ref, vp_ref, vc_ref, vn_ref, bias_ref, o_ref, kwin, vwin, *, rows):
    i = pl.program_id(1)
    kwin[0:NA_TILE] = kp_ref[0]
    kwin[NA_TILE:2 * NA_TILE] = kc_ref[0]
    kwin[2 * NA_TILE:3 * NA_TILE] = kn_ref[0]
    vwin[0:NA_TILE] = vp_ref[0]
    vwin[NA_TILE:2 * NA_TILE] = vc_ref[0]
    vwin[2 * NA_TILE:3 * NA_TILE] = vn_ref[0]
    pair = 2 * HEAD_DIM
    row_id = lax.broadcasted_iota(jnp.int32, (2 * GRID_W, pair), 0)
    lane_id = lax.broadcasted_iota(jnp.int32, (2 * GRID_W, pair), 1)
    own = (row_id < GRID_W) == (lane_id < HEAD_DIM)
    low = lax.broadcasted_iota(jnp.int32, (GRID_W, pair), 1) < HEAD_DIM

    def query_row(a, carry):
        r = i * NA_TILE_ROWS + a
        rs = jnp.clip(r - NB_ROWS // 2, 0, rows - NB_ROWS)
        start = pl.multiple_of((rs - (i - 1) * NA_TILE_ROWS) * GRID_W, GRID_W)
        variant = rs - r + NB_ROWS - 1
        q_rows = pl.ds(pl.multiple_of(a * GRID_W, GRID_W), GRID_W)
        scores = []
        for hp in range(B_HEADS // 2):
            cs = slice(hp * pair, (hp + 1) * pair)
            q2 = q_ref[0, q_rows, cs]
            qs = jnp.where(own, jnp.concatenate([q2, q2], axis=0), jnp.zeros((2 * GRID_W, pair), q2.dtype))
            s = lax.dot_general(qs, kwin[pl.ds(start, NA_KEYS), cs], (((1,), (1,)), ((), ())),
                                preferred_element_type=F32)
            scores.append(s + bias_ref[variant, hp])
        probs = []
        for s in scores:
            p = jnp.exp(s - jnp.max(s, axis=-1, keepdims=True))
            probs.append((p, jnp.sum(p, axis=-1, keepdims=True)))
        outs = []
        for hp, (p, den) in enumerate(probs):
            cs = slice(hp * pair, (hp + 1) * pair)
            o = jnp.dot(p.astype(BF16), vwin[pl.ds(start, NA_KEYS), cs], preferred_element_type=F32) / den
            outs.append(jnp.where(low, o[0:GRID_W], o[GRID_W:]))
        o_ref[0, q_rows, :] = jnp.concatenate(outs, axis=1).astype(o_ref.dtype)
        return carry

    lax.fori_loop(0, NA_TILE_ROWS, query_row, 0)


def _neighbourhood(qb, kb, vb, bias):
    bn, s_len, _ = qb.shape
    rows = s_len // GRID_W
    n_tiles = rows // NA_TILE_ROWS
    cur = pl.BlockSpec((1, NA_TILE, B_WIDTH), lambda b, i: (b, i, 0))
    prev = pl.BlockSpec((1, NA_TILE, B_WIDTH), lambda b, i: (b, jnp.maximum(i - 1, 0), 0))
    nxt = pl.BlockSpec((1, NA_TILE, B_WIDTH), lambda b, i: (b, jnp.minimum(i + 1, n_tiles - 1), 0))
    return pl.pallas_call(
        functools.partial(_na_kernel, rows=rows),
        grid=(bn, n_tiles),
        in_specs=[cur, prev, cur, nxt, prev, cur, nxt, _resident(bias.shape)],
        out_specs=cur,
        out_shape=jax.ShapeDtypeStruct((bn, s_len, B_WIDTH), BF16),
        scratch_shapes=[pltpu.VMEM((3 * NA_TILE, B_WIDTH), BF16)] * 2,
        compiler_params=_params(("parallel", "parallel")),
        name="neighbourhood",
    )(qb, kb, kb, kb, vb, vb, vb, bias)


def _merge_kernel(x_ref, o1_ref, l1_ref, o2_ref, l2_ref, o3_ref, l3_ref, ob_ref, ga_ref, gb_ref,
                  wpa_ref, wpb_ref, wout_ref, gffn_ref, wq_ref, h_ref, u_ref, qp_ref):
    l1, l2, l3 = l1_ref[...], l2_ref[...], l3_ref[...]
    m = jnp.maximum(jnp.maximum(l1, l2), l3)
    w1, w2, w3 = jnp.exp(l1 - m), jnp.exp(l2 - m), jnp.exp(l3 - m)
    oa = (w1 * o1_ref[...] + w2 * o2_ref[...] + w3 * o3_ref[...]) / (w1 + w2 + w3)
    pa = jnp.dot(oa.astype(BF16), wpa_ref[...], preferred_element_type=F32)
    pb = jnp.dot(ob_ref[...], wpb_ref[...], preferred_element_type=F32)
    merged = jax.nn.sigmoid(ga_ref[...].astype(F32)) * pa + jax.nn.sigmoid(gb_ref[...].astype(F32)) * pb
    h = x_ref[...] + jnp.dot(merged.astype(BF16), wout_ref[...], preferred_element_type=F32)
    u = _rms(h, gffn_ref[...])
    for s in range(SUBLANES):
        h_ref[:, s, :] = h[:, s * LANES:(s + 1) * LANES]
        u_ref[:, s, :] = u[:, s * LANES:(s + 1) * LANES]
    qp_ref[...] = jnp.dot(u.astype(BF16), wq_ref[...], preferred_element_type=F32).astype(qp_ref.dtype)


def _merge(x2, dil_outs, ob, ga, gb, wpa, wpb, wout, gffn, wq, tm):
    n_tok = x2.shape[0]
    tok = lambda w: pl.BlockSpec((tm, w), lambda i: (i, 0))
    ins = [x2]
    specs = [tok(D_MODEL)]
    for o, l in dil_outs:
        ins += [o, l]
        specs += [tok(A_OUT_WIDTH), tok(A_OUT_WIDTH)]
    ins += [ob, ga, gb, wpa, wpb, wout, gffn, wq]
    specs += [tok(B_WIDTH), tok(D_MODEL), tok(D_MODEL), _resident(wpa.shape), _resident(wpb.shape),
              _resident(wout.shape), _resident(gffn.shape), _resident(wq.shape)]
    qw = PEER_HEADS * PEER_QDIM
    tile = pl.BlockSpec((tm, SUBLANES, LANES), lambda i: (i, 0, 0))
    return pl.pallas_call(
        _merge_kernel,
        grid=(n_tok // tm,),
        in_specs=specs,
        out_specs=[tile, tile, tok(qw)],
        out_shape=[jax.ShapeDtypeStruct((n_tok, SUBLANES, LANES), F32)] * 2 + [jax.ShapeDtypeStruct((n_tok, qw), BF16)],
        compiler_params=_params(("parallel",)),
        name="merge",
    )(*ins)


def _take_top(problems, count):
    state = [s for s, _, _, _ in problems]
    for j in range(count):
        for i, (_, val_ref, pick_ref, payload) in enumerate(problems):
            s = state[i]
            iota = lax.broadcasted_iota(jnp.int32, s.shape, 0).astype(F32)
            m = jnp.max(s, axis=0, keepdims=True)
            pos = jnp.min(jnp.where(s == m, iota, float(s.shape[0])), axis=0, keepdims=True)
            hit = iota == pos
            val_ref[j:j + 1, :] = m
            pick_ref[j:j + 1, :] = (pos if payload is None
                                    else jnp.max(jnp.where(hit, payload, -1.0), axis=0, keepdims=True))
            state[i] = jnp.where(hit, -jnp.inf, s)


PAIR_COUNTS = tuple(PEER_TOPK // (a + 1) for a in range(PEER_TOPK))
N_PAIRS = sum(PAIR_COUNTS)
PAIR_ROWS = -(-N_PAIRS // SUBLANES) * SUBLANES


TOPK_HEADS_PER_STEP = 8


def _topk_kernel(qp_ref, sk_ref, row_tm_ref, sh_tm_ref, sh_ref, gate_ref,
                 row_scr, sh_scr, va_ref, ia_ref, vb_ref, ib_ref, cand_ref, cidx_ref, best_ref, pick_ref):
    tm = qp_ref.shape[0]
    step = pl.program_id(1)
    stage1 = []
    for hh in range(TOPK_HEADS_PER_STEP):
        for p, (val_ref, idx_ref) in enumerate(((va_ref, ia_ref), (vb_ref, ib_ref))):
            c0 = hh * PEER_QDIM + p * PEER_HALF
            q = qp_ref[:, c0:c0 + PEER_HALF]
            s = lax.dot_general(sk_ref[p], q, (((1,), (1,)), ((), ())), preferred_element_type=F32)
            stage1.append((s, val_ref.at[hh], idx_ref.at[hh], None))
    _take_top(stage1, PEER_TOPK)
    stage2 = []
    for hh in range(TOPK_HEADS_PER_STEP):
        cand_ref[hh, N_PAIRS:, :] = jnp.full((PAIR_ROWS - N_PAIRS, tm), -jnp.inf, F32)
        cidx_ref[hh, N_PAIRS:, :] = jnp.zeros((PAIR_ROWS - N_PAIRS, tm), F32)
        at = 0
        for a, count in enumerate(PAIR_COUNTS):
            cand_ref[hh, at:at + count, :] = va_ref[hh, a:a + 1, :] + vb_ref[hh, 0:count, :]
            cidx_ref[hh, at:at + count, :] = ia_ref[hh, a:a + 1, :] * PEER_NKEYS + ib_ref[hh, 0:count, :]
            at += count
        stage2.append((cand_ref[hh], best_ref.at[hh], pick_ref.at[hh], cidx_ref[hh]))
    _take_top(stage2, PEER_TOPK)
    rows_per_step = TOPK_HEADS_PER_STEP * PEER_TOPK
    best = best_ref[...]
    e = jnp.exp(best - best[:, 0:1])
    gate_ref[0] = (e / jnp.sum(e, axis=1, keepdims=True)).reshape(rows_per_step, tm)
    experts = pick_ref[...].astype(jnp.int32).reshape(rows_per_step, tm)
    shift = jnp.where(experts < HALF_EXPERTS, 16, 0)
    sh_ref[0] = shift
    slots = pl.ds(pl.multiple_of(step * rows_per_step, rows_per_step), rows_per_step)
    row_scr[slots, :] = (experts & (HALF_EXPERTS - 1)) * SUBLANES
    sh_scr[slots, :] = shift

    @pl.when(step == PEER_HEADS // TOPK_HEADS_PER_STEP - 1)
    def _():
        row_tm_ref[0] = row_scr[...].T
        sh_tm_ref[0] = sh_scr[...].T


def _topk(qp, sk_bf16, tm):
    n_tok = qp.shape[0]
    n_tiles = n_tok // tm
    hps = TOPK_HEADS_PER_STEP
    per_step = pl.BlockSpec((1, hps * PEER_TOPK, tm), lambda i, h: (i, h, 0))
    per_tile = pl.BlockSpec((1, tm, PEER_SEL), lambda i, h: (i, 0, 0))
    slot_major = (n_tiles, PEER_SEL, tm)
    token_major = (n_tiles, tm, PEER_SEL)
    return pl.pallas_call(
        _topk_kernel,
        grid=(n_tiles, PEER_HEADS // hps),
        in_specs=[pl.BlockSpec((tm, hps * PEER_QDIM), lambda i, h: (i, h)), _resident(sk_bf16.shape)],
        out_specs=[per_tile, per_tile, per_step, per_step],
        out_shape=[jax.ShapeDtypeStruct(token_major, jnp.int32), jax.ShapeDtypeStruct(token_major, jnp.int32),
                   jax.ShapeDtypeStruct(slot_major, jnp.int32), jax.ShapeDtypeStruct(slot_major, F32)],
        scratch_shapes=[pltpu.VMEM((PEER_SEL, tm), jnp.int32)] * 2
        + [pltpu.VMEM((hps, PEER_TOPK, tm), F32)] * 4 + [pltpu.VMEM((hps, PAIR_ROWS, tm), F32)] * 2
        + [pltpu.VMEM((hps, PEER_TOPK, tm), F32)] * 2,
        compiler_params=_params(("parallel", "arbitrary")),
        name="peer_topk",
    )(qp, sk_bf16)


def _pack_experts(tbl):
    b = lax.bitcast_convert_type(tbl.astype(BF16), jnp.uint16).astype(jnp.uint32)
    w = (b[HALF_EXPERTS:] << 16) | b[:HALF_EXPERTS]
    return lax.bitcast_convert_type(w, jnp.int32).reshape(HALF_EXPERTS * SUBLANES, LANES)


def _table_tile(tbl_ref, offset):
    return tbl_ref[pl.ds(pl.multiple_of(offset, SUBLANES), SUBLANES), :]


PACKED_ROWS = 2 * SUBLANES
V_CHUNK = 32
V_CHUNK_COLS = V_CHUNK * PACKED_ROWS
V_COLS = PEER_SEL * PACKED_ROWS


def _expand_matrix():
    col = np.arange(V_COLS)
    src = np.arange(2 * PEER_SEL)
    e = (src[:, None] % PEER_SEL == col[None, :] // PACKED_ROWS) & (src[:, None] // PEER_SEL == col[None, :] % 2)
    return jnp.asarray(e, BF16)


def _chunk_left(lhs, c):
    c0 = c * V_CHUNK_COLS
    half = V_CHUNK_COLS // 2
    return jnp.concatenate([lhs[:, c0:c0 + half], lhs[:, c0 + half:c0 + V_CHUNK_COLS]], axis=0)


def _chunk_weights(tiles):
    return jnp.concatenate([jnp.concatenate(tiles[:V_CHUNK // 2], axis=0),
                            jnp.concatenate(tiles[V_CHUNK // 2:], axis=0)], axis=1)


def _peer_u_kernel(rows_hbm, x_ref, gate_ref, sh_ref, sh_tm_ref, e_ref, d_ref, tbl_ref, pair_ref,
                   sums_ref, fe_ref, stage_sm, sems):
    tm = x_ref.shape[0]
    lane = lax.broadcasted_iota(jnp.int32, (PEER_SEL, tm), 1)
    low_tm = sh_tm_ref[0] == 16
    flags = jnp.concatenate([low_tm, jnp.logical_not(low_tm)], axis=1)
    fe_ref[...] = jnp.dot(jnp.where(flags, 1.0, 0.0).astype(BF16), e_ref[...], preferred_element_type=F32)
    blockdiag = d_ref[...]
    half_sub = V_CHUNK // 2

    def token(t, offset):
        xb = lax.bitcast_convert_type(x_ref[t].astype(BF16).astype(F32), jnp.int32)
        x2 = pltpu.bitcast(xb | lax.shift_right_logical(xb, 16), BF16)
        lhs = (fe_ref[pl.ds(t, 1), :] * blockdiag).astype(BF16)
        for c in range(PEER_SEL // V_CHUNK):
            tiles = [pltpu.bitcast(_table_tile(tbl_ref, offset(c * V_CHUNK + j)), BF16) * x2
                     for j in range(V_CHUNK)]
            out = jnp.dot(_chunk_left(lhs, c), _chunk_weights(tiles), preferred_element_type=F32)
            at = pl.multiple_of(t * PEER_SEL + c * V_CHUNK, V_CHUNK)
            sums_ref[pl.ds(at, half_sub), :] = out[0:half_sub, 0:LANES]
            sums_ref[pl.ds(at + half_sub, half_sub), :] = out[half_sub:, LANES:]

    _for_staged_tokens(rows_hbm, stage_sm, sems, tm, token)

    def finish(i, act):
        for j in range(SUBLANES):
            t = i * SUBLANES + j
            rows = sums_ref[pl.ds(pl.multiple_of(t * PEER_SEL, PEER_SEL), PEER_SEL), :]
            dots = jnp.sum(rows, axis=-1, keepdims=True)
            act = jnp.where(lane == t, dots, act)
        return act

    act = lax.fori_loop(0, tm // SUBLANES, finish, jnp.zeros((PEER_SEL, tm), F32))
    coef = gate_ref[0] * jax.nn.gelu(act)
    low = sh_ref[0] == 16
    pair_ref[:, 0:PEER_SEL] = jnp.where(low, coef, 0.0).T
    pair_ref[:, PEER_SEL:] = jnp.where(low, 0.0, coef).T


STAGE_TOKENS = SUBLANES
STAGE_SLOTS = 8


def _for_staged_tokens(rows_hbm, stage_sm, sems, tm, token_fn):
    tile = pl.program_id(0)
    n_blocks = tm // STAGE_TOKENS

    def copy(block, slot):
        rows = pl.ds(pl.multiple_of(block * STAGE_TOKENS, STAGE_TOKENS), STAGE_TOKENS)
        return pltpu.make_async_copy(rows_hbm.at[tile, rows], stage_sm.at[slot], sems.at[slot])

    for slot in range(STAGE_SLOTS):
        copy(slot, slot).start()

    def round_(i, carry):
        for slot in range(STAGE_SLOTS):
            block = i * STAGE_SLOTS + slot
            copy(block, slot).wait()
            for j in range(STAGE_TOKENS):
                token_fn(block * STAGE_TOKENS + j, lambda k, slot=slot, j=j: stage_sm[slot, j, k])
            copy(jnp.minimum(block + STAGE_SLOTS, n_blocks - 1), slot).start()
        return carry

    lax.fori_loop(0, n_blocks // STAGE_SLOTS, round_, 0)
    for slot in range(STAGE_SLOTS):
        copy(n_blocks - 1, slot).wait()


def _peer_v_kernel(rows_hbm, c2_ref, h_ref, g_ref, e_ref, d_ref, tbl_ref, y_ref, ce_ref, acc_ref, stage_sm, sems):
    tm = h_ref.shape[0]
    c2 = c2_ref[...]
    c_hi = c2.astype(BF16)
    c_lo = (c2 - c_hi.astype(F32)).astype(BF16)
    ce_ref[0] = jnp.dot(c_hi, e_ref[...], preferred_element_type=F32)
    ce_ref[1] = jnp.dot(c_lo, e_ref[...], preferred_element_type=F32)
    diag = d_ref[...]

    def token(t, offset):
        lhs = jnp.concatenate([ce_ref[0, pl.ds(t, 1), :] * diag, ce_ref[1, pl.ds(t, 1), :] * diag],
                              axis=0).astype(BF16)
        acc = None
        for c in range(PEER_SEL // V_CHUNK):
            tiles = [pltpu.bitcast(_table_tile(tbl_ref, offset(c * V_CHUNK + j)), BF16) for j in range(V_CHUNK)]
            out = jnp.dot(_chunk_left(lhs, c), _chunk_weights(tiles), preferred_element_type=F32)
            acc = out if acc is None else acc + out
        first = acc[0:SUBLANES, 0:LANES] + acc[SUBLANES:2 * SUBLANES, 0:LANES]
        second = acc[2 * SUBLANES:3 * SUBLANES, LANES:] + acc[3 * SUBLANES:, LANES:]
        acc_ref[t] = h_ref[t] + (first + second)

    _for_staged_tokens(rows_hbm, stage_sm, sems, tm, token)
    h = acc_ref[...]
    ms = jnp.sum(jnp.sum(h * h, axis=2, keepdims=True), axis=1, keepdims=True) * (1.0 / D_MODEL)
    acc_ref[...] = h * lax.rsqrt(ms + NORM_EPS) * g_ref[...]
    for s in range(SUBLANES):
        y_ref[:, s * LANES:(s + 1) * LANES] = acc_ref[:, s, :]


def _peer_u(rows_tm, shifts_tm, gates, shifts, u3, tbl, tm):
    n_tiles = gates.shape[0]
    vm = pl.BlockSpec((1, PEER_SEL, tm), lambda i: (i, 0, 0))
    expand = _expand_matrix()
    col = np.arange(V_COLS)
    blockdiag = jnp.asarray((col[None, :] // PACKED_ROWS) % (V_CHUNK // 2) == np.arange(V_CHUNK // 2)[:, None], F32)
    return pl.pallas_call(
        _peer_u_kernel,
        grid=(n_tiles,),
        in_specs=[pl.BlockSpec(memory_space=pl.ANY), pl.BlockSpec((tm, SUBLANES, LANES), lambda i: (i, 0, 0)), vm, vm,
                  pl.BlockSpec((1, tm, PEER_SEL), lambda i: (i, 0, 0)),
                  _resident(expand.shape), _resident(blockdiag.shape), _resident(tbl.shape)],
        out_specs=pl.BlockSpec((tm, 2 * PEER_SEL), lambda i: (i, 0)),
        out_shape=jax.ShapeDtypeStruct((n_tiles * tm, 2 * PEER_SEL), F32),
        scratch_shapes=[pltpu.VMEM((tm * PEER_SEL, LANES), F32), pltpu.VMEM((tm, V_COLS), F32),
                        pltpu.SMEM((STAGE_SLOTS, STAGE_TOKENS, PEER_SEL), jnp.int32),
                        pltpu.SemaphoreType.DMA((STAGE_SLOTS,))],
        compiler_params=_params(("arbitrary",)),
        name="peer_u",
    )(rows_tm, u3, gates, shifts, shifts_tm, expand, blockdiag, tbl)


def _peer_v(rows, coef_pairs, h3, g_final, tbl, tm):
    n_tiles = h3.shape[0] // tm
    tok = pl.BlockSpec((tm, SUBLANES, LANES), lambda i: (i, 0, 0))
    expand = _expand_matrix()
    diag = jnp.asarray((np.arange(V_COLS)[None, :] % PACKED_ROWS) // 2 == np.arange(SUBLANES)[:, None], F32)
    return pl.pallas_call(
        _peer_v_kernel,
        grid=(n_tiles,),
        in_specs=[pl.BlockSpec(memory_space=pl.ANY), pl.BlockSpec((tm, 2 * PEER_SEL), lambda i: (i, 0)), tok,
                  _resident(g_final.shape), _resident(expand.shape), _resident(diag.shape), _resident(tbl.shape)],
        out_specs=pl.BlockSpec((tm, D_MODEL), lambda i: (i, 0)),
        out_shape=jax.ShapeDtypeStruct((h3.shape[0], D_MODEL), F32),
        scratch_shapes=[pltpu.VMEM((2, tm, V_COLS), F32), pltpu.VMEM((tm, SUBLANES, LANES), F32),
                        pltpu.SMEM((STAGE_SLOTS, STAGE_TOKENS, PEER_SEL), jnp.int32),
                        pltpu.SemaphoreType.DMA((STAGE_SLOTS,))],
        compiler_params=_params(("arbitrary",)),
        name="peer_v",
    )(rows, coef_pairs, h3, g_final, expand, diag, tbl)


PROJ_TILE = 512
MERGE_TILE = 256
PEER_TILE = 128


def _trunk(x, p):
    bn, s_len, d = x.shape
    n_tok = bn * s_len
    x2 = x.reshape(n_tok, d)
    *qkv_groups, qb, kb, vb, ga, gb = _inproj(x2, p["norm_mix"], p["w_in"], PROJ_TILE)
    seq = lambda t: t.reshape(bn, s_len, t.shape[-1])
    dil_outs = []
    for group, (window, dilation) in enumerate(DIL_GROUPS):
        assert window // (2 * dilation) == RAD
        sub_len = s_len // dilation
        assert s_len % dilation == 0 and sub_len % RAD == 0
        tq = min(256, sub_len)
        assert sub_len % tq == 0
        bias = _dilated_bias(p["t5_table"], group, dilation, tq)
        o, l = _dilated_group(seq(qkv_groups[group]), bias, group, dilation, tq)
        dil_outs.append((o.reshape(n_tok, A_OUT_WIDTH), l.reshape(n_tok, A_OUT_WIDTH)))
    assert s_len % NA_TILE == 0 and s_len // GRID_W >= NB_ROWS
    ob = _neighbourhood(seq(qb), seq(kb), seq(vb), p["na_bias"]).reshape(n_tok, B_WIDTH)
    h, u, qp = _merge(x2, dil_outs, ob, ga, gb, p["w_proj_a"], p["w_proj_b"], p["w_out"], p["norm_ffn"],
                      p["w_query"], MERGE_TILE)
    rows_tm, shifts_tm, shifts, gates = _topk(qp, p["sub_keys"], PEER_TILE)
    coef_pairs = _peer_u(rows_tm, shifts_tm, gates, shifts, u, p["expert_u"], PEER_TILE)
    y = _peer_v(rows_tm, coef_pairs, h, p["norm_final"], p["expert_v"], PEER_TILE)
    return y.reshape(bn, s_len, d)


def kernel(x_prompt, x_sample, norm_mix, w_in, w_proj_a, w_proj_b, w_out, rpb, norm_ffn, w_query, sub_keys,
           expert_u, expert_v, t5_table, norm_final):
    assert norm_mix.shape[0] == 1, "single-layer model"
    p = {
        "norm_mix": norm_mix[0].reshape(1, D_MODEL).astype(F32),
        "w_in": _arrange_w_in(w_in[0]),
        "w_proj_a": w_proj_a[0].astype(BF16),
        "w_proj_b": w_proj_b[0].astype(BF16),
        "w_out": w_out[0].astype(BF16),
        "na_bias": _na_bias(rpb[0]),
        "norm_ffn": norm_ffn[0].reshape(1, D_MODEL).astype(F32),
        "w_query": w_query[0].astype(BF16),
        "sub_keys": sub_keys[0].astype(BF16),
        "expert_u": _pack_experts(expert_u[0]),
        "expert_v": _pack_experts(expert_v[0]),
        "t5_table": t5_table,
        "norm_final": norm_final.reshape(SUBLANES, LANES).astype(F32),
    }
    return (_trunk(x_prompt, p), _trunk(x_sample, p))
```

```python
import functools
import math

import numpy as np
import jax
import jax.numpy as jnp
from jax import lax
from jax.experimental import pallas as pl
from jax.experimental.pallas import tpu as pltpu

D_MODEL = 1024
HEAD_DIM = 64
DIL_GROUPS = ((128, 1), (512, 4), (2048, 16))
A_HEADS_PER_GROUP = 4
A_HEADS = A_HEADS_PER_GROUP * len(DIL_GROUPS)
A_WIDTH = A_HEADS * HEAD_DIM
A_OUT_WIDTH = A_HEADS_PER_GROUP * HEAD_DIM
T5_BUCKETS = 32
T5_MAX_DIST = 1024
GRID_W = 64
B_HEADS = 8
B_WIDTH = B_HEADS * HEAD_DIM
NB_ROWS = 8
NB_COLS = 16
PEER_HEADS = 8
PEER_NKEYS = 128
PEER_EXPERTS = PEER_NKEYS * PEER_NKEYS
PEER_QDIM = 256
PEER_HALF = PEER_QDIM // 2
PEER_TOPK = 16
PEER_SEL = PEER_HEADS * PEER_TOPK
NORM_EPS = 1e-6
NEG_INF = -1e30

RAD = 64
SUBLANES = 8
LANES = 128
HALF_EXPERTS = PEER_EXPERTS // 2
VMEM_LIMIT = 48 * 1024 * 1024

F32 = jnp.float32
BF16 = jnp.bfloat16


def _resident(shape):
    nd = len(shape)
    return pl.BlockSpec(shape, lambda *_: (0,) * nd, pipeline_mode=pl.Buffered(1))


def _params(sem):
    return pltpu.CompilerParams(dimension_semantics=sem, vmem_limit_bytes=VMEM_LIMIT)


IN_SPLITS = (A_WIDTH, A_WIDTH, A_WIDTH, B_WIDTH, B_WIDTH, B_WIDTH, D_MODEL, D_MODEL)
QK_SCALE = HEAD_DIM ** -0.5


def _arrange_w_in(w_in):
    qa, ka, va, qb, rest = jnp.split(w_in, [A_WIDTH, 2 * A_WIDTH, 3 * A_WIDTH, 3 * A_WIDTH + B_WIDTH], axis=1)
    groups = []
    for g in range(len(DIL_GROUPS)):
        cs = slice(g * A_OUT_WIDTH, (g + 1) * A_OUT_WIDTH)
        groups += [qa[:, cs] * QK_SCALE, ka[:, cs], va[:, cs]]
    return jnp.concatenate(groups + [qb * QK_SCALE, rest], axis=1).astype(BF16)


def _rms(x, g):
    return x * lax.rsqrt(jnp.mean(x * x, axis=-1, keepdims=True) + NORM_EPS) * g


def _inproj_kernel(x_ref, g_ref, w_ref, *out_refs):
    u = _rms(x_ref[...], g_ref[...]).astype(BF16)
    off = 0
    for o_ref, width in zip(out_refs, IN_SPLITS):
        o_ref[...] = jnp.dot(u, w_ref[:, off:off + width], preferred_element_type=F32).astype(o_ref.dtype)
        off += width


def _inproj(x2, g, w_bf16, tm):
    n_tok = x2.shape[0]
    return pl.pallas_call(
        _inproj_kernel,
        grid=(n_tok // tm,),
        in_specs=[pl.BlockSpec((tm, D_MODEL), lambda i: (i, 0)),
                  _resident((1, D_MODEL)),
                  _resident(w_bf16.shape)],
        out_specs=[pl.BlockSpec((tm, w), lambda i: (i, 0)) for w in IN_SPLITS],
        out_shape=[jax.ShapeDtypeStruct((n_tok, w), BF16) for w in IN_SPLITS],
        compiler_params=_params(("parallel",)),
        name="inproj",
    )(x2, g, w_bf16)


def _t5_bucket(rel):
    n = -rel
    nb = T5_BUCKETS // 2
    ret = (n < 0).astype(np.int32) * nb
    n = np.abs(n)
    max_exact = nb // 2
    large = max_exact + (np.log(np.maximum(n, 1) / max_exact) / math.log(T5_MAX_DIST / max_exact)
                         * (nb - max_exact)).astype(np.int32)
    large = np.minimum(large, nb - 1)
    return (ret + np.where(n < max_exact, n, large)).astype(np.int32)


def _dilated_bias(t5_table, group, dilation, tq):
    rel = np.arange(tq + 2 * RAD)[None, :] - RAD - np.arange(tq)[:, None]
    tab = t5_table[:, group * A_HEADS_PER_GROUP:(group + 1) * A_HEADS_PER_GROUP].astype(F32)
    in_band = np.abs(rel) <= RAD
    bucket = np.where(in_band, _t5_bucket(rel * dilation), -1)
    bias = jnp.full((A_HEADS_PER_GROUP,) + rel.shape, NEG_INF, F32)
    for b in np.unique(bucket[in_band]):
        bias = jnp.where(jnp.asarray(bucket == b)[None], tab[int(b)][:, None, None], bias)
    return bias


def _dilated_kernel(q_ref, kp_ref, kc_ref, kn_ref, vp_ref, vc_ref, vn_ref, bias_ref, o_ref, l_ref, *, tq, n_tiles):
    i = pl.program_id(2)
    q = q_ref[0]
    k_all = jnp.concatenate([kp_ref[0], kc_ref[0], kn_ref[0]], axis=0)
    v_all = jnp.concatenate([vp_ref[0], vc_ref[0], vn_ref[0]], axis=0)
    nk = tq + 2 * RAD
    col = lax.broadcasted_iota(jnp.int32, (tq, nk), 1)
    first_valid = jnp.where(i == 0, RAD, 0)
    end_valid = jnp.where(i == n_tiles - 1, tq + RAD, nk)
    valid = (col >= first_valid) & (col < end_valid)
    lane = lax.broadcasted_iota(jnp.int32, (tq, A_OUT_WIDTH), 1)
    o_acc = jnp.zeros((tq, A_OUT_WIDTH), F32)
    l_acc = jnp.zeros((tq, A_OUT_WIDTH), F32)
    for h in range(A_HEADS_PER_GROUP):
        head = (lane >= h * HEAD_DIM) & (lane < (h + 1) * HEAD_DIM)
        qh = jnp.where(head, q, jnp.zeros_like(q))
        s = lax.dot_general(qh, k_all, (((1,), (1,)), ((), ())), preferred_element_type=F32)
        s = jnp.where(valid, s + bias_ref[h], NEG_INF)
        m = jnp.max(s, axis=-1, keepdims=True)
        p = jnp.exp(s - m)
        den = jnp.sum(p, axis=-1, keepdims=True)
        o = jnp.dot(p.astype(BF16), v_all, preferred_element_type=F32) / den
        lse = m + jnp.log(den)
        o_acc = jnp.where(head, o, o_acc)
        l_acc = jnp.where(head, lse, l_acc)
    o_ref[0] = o_acc
    l_ref[0] = l_acc


def _dilated_group(qkv, bias, group, dilation, tq):
    bn, s_len, _ = qkv.shape
    sub_len = s_len // dilation
    n_tiles = sub_len // tq
    blk = tq // RAD
    n_rad = sub_len // RAD
    view = qkv.reshape(bn, sub_len, dilation * A_WIDTH)

    def specs(part):
        cur = pl.BlockSpec((1, tq, A_OUT_WIDTH), lambda b, r, i: (b, i, r * 3 + part))
        prev = pl.BlockSpec((1, RAD, A_OUT_WIDTH), lambda b, r, i: (b, jnp.maximum(i * blk - 1, 0), r * 3 + part))
        nxt = pl.BlockSpec((1, RAD, A_OUT_WIDTH),
                           lambda b, r, i: (b, jnp.minimum((i + 1) * blk, n_rad - 1), r * 3 + part))
        return cur, prev, nxt

    out = pl.BlockSpec((1, tq, A_OUT_WIDTH), lambda b, r, i: (b, i, r))
    (q_cur, _, _), (k_cur, k_prev, k_next), (v_cur, v_prev, v_next) = specs(0), specs(1), specs(2)
    o, l = pl.pallas_call(
        functools.partial(_dilated_kernel, tq=tq, n_tiles=n_tiles),
        grid=(bn, dilation, n_tiles),
        in_specs=[q_cur, k_prev, k_cur, k_next, v_prev, v_cur, v_next, _resident(bias.shape)],
        out_specs=[out, out],
        out_shape=[jax.ShapeDtypeStruct((bn, sub_len, dilation * A_OUT_WIDTH), F32)] * 2,
        compiler_params=_params(("parallel", "parallel", "parallel")),
        name=f"dilated{group}",
    )(view, view, view, view, view, view, view, bias)
    return o.reshape(bn, s_len, A_OUT_WIDTH), l.reshape(bn, s_len, A_OUT_WIDTH)


NA_TILE_ROWS = 8
NA_TILE = NA_TILE_ROWS * GRID_W
NA_KEYS = NB_ROWS * GRID_W


def _na_bias(rpb):
    cols = np.arange(GRID_W)
    col_start = np.clip(cols - NB_COLS // 2, 0, GRID_W - NB_COLS)
    col_valid = (cols[None, :] >= col_start[:, None]) & (cols[None, :] < col_start[:, None] + NB_COLS)
    dc = np.clip(cols[None, :] - cols[:, None] + NB_COLS - 1, 0, 2 * NB_COLS - 2)
    rpb = rpb.astype(F32)
    variants = []
    for v in range(NB_ROWS):
        dr = np.clip(np.arange(NB_ROWS) + v, 0, 2 * NB_ROWS - 2)
        tab = rpb[:, dr][:, :, dc]
        tab = jnp.where(jnp.asarray(col_valid)[None, None], tab, NEG_INF)
        variants.append(jnp.transpose(tab, (0, 2, 1, 3)).reshape(B_HEADS // 2, 2 * GRID_W, NA_KEYS))
    return jnp.stack(variants, axis=0)


def _na_kernel(q_ref, kp_ref, kc_ref, kn_ref, vp_ref, vc_ref, vn_ref, bias_ref, o_ref, kwin, vwin, *, rows):
    i = pl.program_id(1)
    kwin[0:NA_TILE] = kp_ref[0]
    kwin[NA_TILE:2 * NA_TILE] = kc_ref[0]
    kwin[2 * NA_TILE:3 * NA_TILE] = kn_ref[0]
    vwin[0:NA_TILE] = vp_ref[0]
    vwin[NA_TILE:2 * NA_TILE] = vc_ref[0]
    vwin[2 * NA_TILE:3 * NA_TILE] = vn_ref[0]
    pair = 2 * HEAD_DIM
    row_id = lax.broadcasted_iota(jnp.int32, (2 * GRID_W, pair), 0)
    lane_id = lax.broadcasted_iota(jnp.int32, (2 * GRID_W, pair), 1)
    own = (row_id < GRID_W) == (lane_id < HEAD_DIM)
    low = lax.broadcasted_iota(jnp.int32, (GRID_W, pair), 1) < HEAD_DIM

    def query_row(a, carry):
        r = i * NA_TILE_ROWS + a
        rs = jnp.clip(r - NB_ROWS // 2, 0, rows - NB_ROWS)
        start = pl.multiple_of((rs - (i - 1) * NA_TILE_ROWS) * GRID_W, GRID_W)
        variant = rs - r + NB_ROWS - 1
        q_rows = pl.ds(pl.multiple_of(a * GRID_W, GRID_W), GRID_W)
        scores = []
        for hp in range(B_HEADS // 2):
            cs = slice(hp * pair, (hp + 1) * pair)
            q2 = q_ref[0, q_rows, cs]
            qs = jnp.where(own, jnp.concatenate([q2, q2], axis=0), jnp.zeros((2 * GRID_W, pair), q2.dtype))
            s = lax.dot_general(qs, kwin[pl.ds(start, NA_KEYS), cs], (((1,), (1,)), ((), ())),
                                preferred_element_type=F32)
            scores.append(s + bias_ref[variant, hp])
        probs = []
        for s in scores:
            p = jnp.exp(s - jnp.max(s, axis=-1, keepdims=True))
            probs.append((p, jnp.sum(p, axis=-1, keepdims=True)))
        outs = []
        for hp, (p, den) in enumerate(probs):
            cs = slice(hp * pair, (hp + 1) * pair)
            o = jnp.dot(p.astype(BF16), vwin[pl.ds(start, NA_KEYS), cs], preferred_element_type=F32) / den
            outs.append(jnp.where(low, o[0:GRID_W], o[GRID_W:]))
        o_ref[0, q_rows, :] = jnp.concatenate(outs, axis=1).astype(o_ref.dtype)
        return carry

    lax.fori_loop(0, NA_TILE_ROWS, query_row, 0)


def _neighbourhood(qb, kb, vb, bias):
    bn, s_len, _ = qb.shape
    rows = s_len // GRID_W
    n_tiles = rows // NA_TILE_ROWS
    cur = pl.BlockSpec((1, NA_TILE, B_WIDTH), lambda b, i: (b, i, 0))
    prev = pl.BlockSpec((1, NA_TILE, B_WIDTH), lambda b, i: (b, jnp.maximum(i - 1, 0), 0))
    nxt = pl.BlockSpec((1, NA_TILE, B_WIDTH), lambda b, i: (b, jnp.minimum(i + 1, n_tiles - 1), 0))
    return pl.pallas_call(
        functools.partial(_na_kernel, rows=rows),
        grid=(bn, n_tiles),
        in_specs=[cur, prev, cur, nxt, prev, cur, nxt, _resident(bias.shape)],
        out_specs=cur,
        out_shape=jax.ShapeDtypeStruct((bn, s_len, B_WIDTH), BF16),
        scratch_shapes=[pltpu.VMEM((3 * NA_TILE, B_WIDTH), BF16)] * 2,
        compiler_params=_params(("parallel", "parallel")),
        name="neighbourhood",
    )(qb, kb, kb, kb, vb, vb, vb, bias)


def _merge_kernel(x_ref, o1_ref, l1_ref, o2_ref, l2_ref, o3_ref, l3_ref, ob_ref, ga_ref, gb_ref,
                  wpa_ref, wpb_ref, wout_ref, gffn_ref, wq_ref, h_ref, u_ref, qp_ref):
    l1, l2, l3 = l1_ref[...], l2_ref[...], l3_ref[...]
    m = jnp.maximum(jnp.maximum(l1, l2), l3)
    w1, w2, w3 = jnp.exp(l1 - m), jnp.exp(l2 - m), jnp.exp(l3 - m)
    oa = (w1 * o1_ref[...] + w2 * o2_ref[...] + w3 * o3_ref[...]) / (w1 + w2 + w3)
    pa = jnp.dot(oa.astype(BF16), wpa_ref[...], preferred_element_type=F32)
    pb = jnp.dot(ob_ref[...], wpb_ref[...], preferred_element_type=F32)
    merged = jax.nn.sigmoid(ga_ref[...].astype(F32)) * pa + jax.nn.sigmoid(gb_ref[...].astype(F32)) * pb
    h = x_ref[...] + jnp.dot(merged.astype(BF16), wout_ref[...], preferred_element_type=F32)
    u = _rms(h, gffn_ref[...])
    for s in range(SUBLANES):
        h_ref[:, s, :] = h[:, s * LANES:(s + 1) * LANES]
        u_ref[:, s, :] = u[:, s * LANES:(s + 1) * LANES]
    qp_ref[...] = jnp.dot(u.astype(BF16), wq_ref[...], preferred_element_type=F32).astype(qp_ref.dtype)


def _merge(x2, dil_outs, ob, ga, gb, wpa, wpb, wout, gffn, wq, tm):
    n_tok = x2.shape[0]
    tok = lambda w: pl.BlockSpec((tm, w), lambda i: (i, 0))
    ins = [x2]
    specs = [tok(D_MODEL)]
    for o, l in dil_outs:
        ins += [o, l]
        specs += [tok(A_OUT_WIDTH), tok(A_OUT_WIDTH)]
    ins += [ob, ga, gb, wpa, wpb, wout, gffn, wq]
    specs += [tok(B_WIDTH), tok(D_MODEL), tok(D_MODEL), _resident(wpa.shape), _resident(wpb.shape),
              _resident(wout.shape), _resident(gffn.shape), _resident(wq.shape)]
    qw = PEER_HEADS * PEER_QDIM
    tile = pl.BlockSpec((tm, SUBLANES, LANES), lambda i: (i, 0, 0))
    return pl.pallas_call(
        _merge_kernel,
        grid=(n_tok // tm,),
        in_specs=specs,
        out_specs=[tile, tile, tok(qw)],
        out_shape=[jax.ShapeDtypeStruct((n_tok, SUBLANES, LANES), F32)] * 2 + [jax.ShapeDtypeStruct((n_tok, qw), BF16)],
        compiler_params=_params(("parallel",)),
        name="merge",
    )(*ins)


def _take_top(problems, count):
    state = [s for s, _, _, _ in problems]
    for j in range(count):
        for i, (_, val_ref, pick_ref, payload) in enumerate(problems):
            s = state[i]
            iota = lax.broadcasted_iota(jnp.int32, s.shape, 0).astype(F32)
            m = jnp.max(s, axis=0, keepdims=True)
            pos = jnp.min(jnp.where(s == m, iota, float(s.shape[0])), axis=0, keepdims=True)
            hit = iota == pos
            val_ref[j:j + 1, :] = m
            pick_ref[j:j + 1, :] = (pos if payload is None
                                    else jnp.max(jnp.where(hit, payload, -1.0), axis=0, keepdims=True))
            state[i] = jnp.where(hit, -jnp.inf, s)


PAIR_COUNTS = tuple(PEER_TOPK // (a + 1) for a in range(PEER_TOPK))
N_PAIRS = sum(PAIR_COUNTS)
PAIR_ROWS = -(-N_PAIRS // SUBLANES) * SUBLANES


TOPK_HEADS_PER_STEP = 8


def _topk_kernel(qp_ref, sk_ref, row_tm_ref, sh_tm_ref, sh_ref, gate_ref,
                 row_scr, sh_scr, va_ref, ia_ref, vb_ref, ib_ref, cand_ref, cidx_ref, best_ref, pick_ref):
    tm = qp_ref.shape[0]
    step = pl.program_id(1)
    stage1 = []
    for hh in range(TOPK_HEADS_PER_STEP):
        for p, (val_ref, idx_ref) in enumerate(((va_ref, ia_ref), (vb_ref, ib_ref))):
            c0 = hh * PEER_QDIM + p * PEER_HALF
            q = qp_ref[:, c0:c0 + PEER_HALF]
            s = lax.dot_general(sk_ref[p], q, (((1,), (1,)), ((), ())), preferred_element_type=F32)
            stage1.append((s, val_ref.at[hh], idx_ref.at[hh], None))
    _take_top(stage1, PEER_TOPK)
    stage2 = []
    for hh in range(TOPK_HEADS_PER_STEP):
        cand_ref[hh, N_PAIRS:, :] = jnp.full((PAIR_ROWS - N_PAIRS, tm), -jnp.inf, F32)
        cidx_ref[hh, N_PAIRS:, :] = jnp.zeros((PAIR_ROWS - N_PAIRS, tm), F32)
        at = 0
        for a, count in enumerate(PAIR_COUNTS):
            cand_ref[hh, at:at + count, :] = va_ref[hh, a:a + 1, :] + vb_ref[hh, 0:count, :]
            cidx_ref[hh, at:at + count, :] = ia_ref[hh, a:a + 1, :] * PEER_NKEYS + ib_ref[hh, 0:count, :]
            at += count
        stage2.append((cand_ref[hh], best_ref.at[hh], pick_ref.at[hh], cidx_ref[hh]))
    _take_top(stage2, PEER_TOPK)
    rows_per_step = TOPK_HEADS_PER_STEP * PEER_TOPK
    best = best_ref[...]
    e = jnp.exp(best - best[:, 0:1])
    gate_ref[0] = (e / jnp.sum(e, axis=1, keepdims=True)).reshape(rows_per_step, tm)
    experts = pick_ref[...].astype(jnp.int32).reshape(rows_per_step, tm)
    shift = jnp.where(experts < HALF_EXPERTS, 16, 0)
    sh_ref[0] = shift
    slots = pl.ds(pl.multiple_of(step * rows_per_step, rows_per_step), rows_per_step)
    row_scr[slots, :] = (experts & (HALF_EXPERTS - 1)) * SUBLANES
    sh_scr[slots, :] = shift

    @pl.when(step == PEER_HEADS // TOPK_HEADS_PER_STEP - 1)
    def _():
        row_tm_ref[0] = row_scr[...].T
        sh_tm_ref[0] = sh_scr[...].T


def _topk(qp, sk_bf16, tm):
    n_tok = qp.shape[0]
    n_tiles = n_tok // tm
    hps = TOPK_HEADS_PER_STEP
    per_step = pl.BlockSpec((1, hps * PEER_TOPK, tm), lambda i, h: (i, h, 0))
    per_tile = pl.BlockSpec((1, tm, PEER_SEL), lambda i, h: (i, 0, 0))
    slot_major = (n_tiles, PEER_SEL, tm)
    token_major = (n_tiles, tm, PEER_SEL)
    return pl.pallas_call(
        _topk_kernel,
        grid=(n_tiles, PEER_HEADS // hps),
        in_specs=[pl.BlockSpec((tm, hps * PEER_QDIM), lambda i, h: (i, h)), _resident(sk_bf16.shape)],
        out_specs=[per_tile, per_tile, per_step, per_step],
        out_shape=[jax.ShapeDtypeStruct(token_major, jnp.int32), jax.ShapeDtypeStruct(token_major, jnp.int32),
                   jax.ShapeDtypeStruct(slot_major, jnp.int32), jax.ShapeDtypeStruct(slot_major, F32)],
        scratch_shapes=[pltpu.VMEM((PEER_SEL, tm), jnp.int32)] * 2
        + [pltpu.VMEM((hps, PEER_TOPK, tm), F32)] * 4 + [pltpu.VMEM((hps, PAIR_ROWS, tm), F32)] * 2
        + [pltpu.VMEM((hps, PEER_TOPK, tm), F32)] * 2,
        compiler_params=_params(("parallel", "arbitrary")),
        name="peer_topk",
    )(qp, sk_bf16)


def _pack_experts(tbl):
    b = lax.bitcast_convert_type(tbl.astype(BF16), jnp.uint16).astype(jnp.uint32)
    w = (b[HALF_EXPERTS:] << 16) | b[:HALF_EXPERTS]
    return lax.bitcast_convert_type(w, jnp.int32).reshape(HALF_EXPERTS * SUBLANES, LANES)


def _table_tile(tbl_ref, offset):
    return tbl_ref[pl.ds(pl.multiple_of(offset, SUBLANES), SUBLANES), :]


PACKED_ROWS = 2 * SUBLANES
V_CHUNK = 32
V_CHUNK_COLS = V_CHUNK * PACKED_ROWS
V_COLS = PEER_SEL * PACKED_ROWS


def _expand_matrix():
    col = np.arange(V_COLS)
    src = np.arange(2 * PEER_SEL)
    e = (src[:, None] % PEER_SEL == col[None, :] // PACKED_ROWS) & (src[:, None] // PEER_SEL == col[None, :] % 2)
    return jnp.asarray(e, BF16)


def _chunk_left(lhs, c):
    c0 = c * V_CHUNK_COLS
    half = V_CHUNK_COLS // 2
    return jnp.concatenate([lhs[:, c0:c0 + half], lhs[:, c0 + half:c0 + V_CHUNK_COLS]], axis=0)


def _chunk_weights(tiles):
    return jnp.concatenate([jnp.concatenate(tiles[:V_CHUNK // 2], axis=0),
                            jnp.concatenate(tiles[V_CHUNK // 2:], axis=0)], axis=1)


def _peer_u_kernel(rows_hbm, x_ref, gate_ref, sh_ref, sh_tm_ref, e_ref, d_ref, tbl_ref, pair_ref,
                   sums_ref, fe_ref, stage_sm, sems):
    tm = x_ref.shape[0]
    lane = lax.broadcasted_iota(jnp.int32, (PEER_SEL, tm), 1)
    low_tm = sh_tm_ref[0] == 16
    flags = jnp.concatenate([low_tm, jnp.logical_not(low_tm)], axis=1)
    fe_ref[...] = jnp.dot(jnp.where(flags, 1.0, 0.0).astype(BF16), e_ref[...], preferred_element_type=F32)
    blockdiag = d_ref[...]
    half_sub = V_CHUNK // 2

    def token(t, offset):
        xb = lax.bitcast_convert_type(x_ref[t].astype(BF16).astype(F32), jnp.int32)
        x2 = pltpu.bitcast(xb | lax.shift_right_logical(xb, 16), BF16)
        lhs = (fe_ref[pl.ds(t, 1), :] * blockdiag).astype(BF16)
        for c in range(PEER_SEL // V_CHUNK):
            tiles = [pltpu.bitcast(_table_tile(tbl_ref, offset(c * V_CHUNK + j)), BF16) * x2
                     for j in range(V_CHUNK)]
            out = jnp.dot(_chunk_left(lhs, c), _chunk_weights(tiles), preferred_element_type=F32)
            at = pl.multiple_of(t * PEER_SEL + c * V_CHUNK, V_CHUNK)
            sums_ref[pl.ds(at, half_sub), :] = out[0:half_sub, 0:LANES]
            sums_ref[pl.ds(at + half_sub, half_sub), :] = out[half_sub:, LANES:]

    _for_staged_tokens(rows_hbm, stage_sm, sems, tm, token)

    def finish(i, act):
        for j in range(SUBLANES):
            t = i * SUBLANES + j
            rows = sums_ref[pl.ds(pl.multiple_of(t * PEER_SEL, PEER_SEL), PEER_SEL), :]
            dots = jnp.sum(rows, axis=-1, keepdims=True)
            act = jnp.where(lane == t, dots, act)
        return act

    act = lax.fori_loop(0, tm // SUBLANES, finish, jnp.zeros((PEER_SEL, tm), F32))
    coef = gate_ref[0] * jax.nn.gelu(act)
    low = sh_ref[0] == 16
    pair_ref[:, 0:PEER_SEL] = jnp.where(low, coef, 0.0).T
    pair_ref[:, PEER_SEL:] = jnp.where(low, 0.0, coef).T


STAGE_TOKENS = 4 * SUBLANES
STAGE_SLOTS = 2


def _for_staged_tokens(rows_hbm, stage_sm, sems, tm, token_fn):
    tile = pl.program_id(0)
    n_blocks = tm // STAGE_TOKENS
    total = pl.num_programs(0) * n_blocks

    def copy(block, slot):
        return pltpu.make_async_copy(rows_hbm.at[block], stage_sm.at[slot], sems.at[slot])

    @pl.when(tile == 0)
    def _():
        for slot in range(STAGE_SLOTS):
            copy(slot, slot).start()

    def round_(i, carry):
        for slot in range(STAGE_SLOTS):
            local = i * STAGE_SLOTS + slot
            block = tile * n_blocks + local
            copy(block, slot).wait()
            for j in range(STAGE_TOKENS):
                token_fn(local * STAGE_TOKENS + j, lambda k, slot=slot, j=j: stage_sm[slot, j, k])
            copy(jnp.minimum(block + STAGE_SLOTS, total - 1), slot).start()
        return carry

    lax.fori_loop(0, n_blocks // STAGE_SLOTS, round_, 0)

    @pl.when(tile == pl.num_programs(0) - 1)
    def _():
        for slot in range(STAGE_SLOTS):
            copy(total - 1, slot).wait()


def _peer_v_kernel(rows_hbm, c2_ref, h_ref, g_ref, e_ref, d_ref, tbl_ref, y_ref, ce_ref, acc_ref, stage_sm, sems):
    tm = h_ref.shape[0]
    c2 = c2_ref[...]
    c_hi = c2.astype(BF16)
    c_lo = (c2 - c_hi.astype(F32)).astype(BF16)
    ce_ref[0] = jnp.dot(c_hi, e_ref[...], preferred_element_type=F32)
    ce_ref[1] = jnp.dot(c_lo, e_ref[...], preferred_element_type=F32)
    diag = d_ref[...]

    def token(t, offset):
        lhs = jnp.concatenate([ce_ref[0, pl.ds(t, 1), :] * diag, ce_ref[1, pl.ds(t, 1), :] * diag],
                              axis=0).astype(BF16)
        acc = None
        for c in range(PEER_SEL // V_CHUNK):
            tiles = [pltpu.bitcast(_table_tile(tbl_ref, offset(c * V_CHUNK + j)), BF16) for j in range(V_CHUNK)]
            out = jnp.dot(_chunk_left(lhs, c), _chunk_weights(tiles), preferred_element_type=F32)
            acc = out if acc is None else acc + out
        first = acc[0:SUBLANES, 0:LANES] + acc[SUBLANES:2 * SUBLANES, 0:LANES]
        second = acc[2 * SUBLANES:3 * SUBLANES, LANES:] + acc[3 * SUBLANES:, LANES:]
        acc_ref[t] = h_ref[t] + (first + second)

    _for_staged_tokens(rows_hbm, stage_sm, sems, tm, token)
    h = acc_ref[...]
    ms = jnp.sum(jnp.sum(h * h, axis=2, keepdims=True), axis=1, keepdims=True) * (1.0 / D_MODEL)
    acc_ref[...] = h * lax.rsqrt(ms + NORM_EPS) * g_ref[...]
    for s in range(SUBLANES):
        y_ref[:, s * LANES:(s + 1) * LANES] = acc_ref[:, s, :]


def _peer_u(rows_tm, shifts_tm, gates, shifts, u3, tbl, tm):
    n_tiles = gates.shape[0]
    vm = pl.BlockSpec((1, PEER_SEL, tm), lambda i: (i, 0, 0))
    expand = _expand_matrix()
    col = np.arange(V_COLS)
    blockdiag = jnp.asarray((col[None, :] // PACKED_ROWS) % (V_CHUNK // 2) == np.arange(V_CHUNK // 2)[:, None], F32)
    return pl.pallas_call(
        _peer_u_kernel,
        grid=(n_tiles,),
        in_specs=[pl.BlockSpec(memory_space=pl.ANY), pl.BlockSpec((tm, SUBLANES, LANES), lambda i: (i, 0, 0)), vm, vm,
                  pl.BlockSpec((1, tm, PEER_SEL), lambda i: (i, 0, 0)),
                  _resident(expand.shape), _resident(blockdiag.shape), _resident(tbl.shape)],
        out_specs=pl.BlockSpec((tm, 2 * PEER_SEL), lambda i: (i, 0)),
        out_shape=jax.ShapeDtypeStruct((n_tiles * tm, 2 * PEER_SEL), F32),
        scratch_shapes=[pltpu.VMEM((tm * PEER_SEL, LANES), F32), pltpu.VMEM((tm, V_COLS), F32),
                        pltpu.SMEM((STAGE_SLOTS, STAGE_TOKENS, PEER_SEL), jnp.int32),
                        pltpu.SemaphoreType.DMA((STAGE_SLOTS,))],
        compiler_params=_params(("arbitrary",)),
        name="peer_u",
    )(rows_tm, u3, gates, shifts, shifts_tm, expand, blockdiag, tbl)


def _peer_v(rows, coef_pairs, h3, g_final, tbl, tm):
    n_tiles = h3.shape[0] // tm
    tok = pl.BlockSpec((tm, SUBLANES, LANES), lambda i: (i, 0, 0))
    expand = _expand_matrix()
    diag = jnp.asarray((np.arange(V_COLS)[None, :] % PACKED_ROWS) // 2 == np.arange(SUBLANES)[:, None], F32)
    return pl.pallas_call(
        _peer_v_kernel,
        grid=(n_tiles,),
        in_specs=[pl.BlockSpec(memory_space=pl.ANY), pl.BlockSpec((tm, 2 * PEER_SEL), lambda i: (i, 0)), tok,
                  _resident(g_final.shape), _resident(expand.shape), _resident(diag.shape), _resident(tbl.shape)],
        out_specs=pl.BlockSpec((tm, D_MODEL), lambda i: (i, 0)),
        out_shape=jax.ShapeDtypeStruct((h3.shape[0], D_MODEL), F32),
        scratch_shapes=[pltpu.VMEM((2, tm, V_COLS), F32), pltpu.VMEM((tm, SUBLANES, LANES), F32),
                        pltpu.SMEM((STAGE_SLOTS, STAGE_TOKENS, PEER_SEL), jnp.int32),
                        pltpu.SemaphoreType.DMA((STAGE_SLOTS,))],
        compiler_params=_params(("arbitrary",)),
        name="peer_v",
    )(rows, coef_pairs, h3, g_final, expand, diag, tbl)


PROJ_TILE = 512
MERGE_TILE = 256
PEER_TILE = 128


def _trunk(x, p):
    bn, s_len, d = x.shape
    n_tok = bn * s_len
    x2 = x.reshape(n_tok, d)
    *qkv_groups, qb, kb, vb, ga, gb = _inproj(x2, p["norm_mix"], p["w_in"], PROJ_TILE)
    seq = lambda t: t.reshape(bn, s_len, t.shape[-1])
    dil_outs = []
    for group, (window, dilation) in enumerate(DIL_GROUPS):
        assert window // (2 * dilation) == RAD
        sub_len = s_len // dilation
        assert s_len % dilation == 0 and sub_len % RAD == 0
        tq = min(256, sub_len)
        assert sub_len % tq == 0
        bias = _dilated_bias(p["t5_table"], group, dilation, tq)
        o, l = _dilated_group(seq(qkv_groups[group]), bias, group, dilation, tq)
        dil_outs.append((o.reshape(n_tok, A_OUT_WIDTH), l.reshape(n_tok, A_OUT_WIDTH)))
    assert s_len % NA_TILE == 0 and s_len // GRID_W >= NB_ROWS
    ob = _neighbourhood(seq(qb), seq(kb), seq(vb), p["na_bias"]).reshape(n_tok, B_WIDTH)
    h, u, qp = _merge(x2, dil_outs, ob, ga, gb, p["w_proj_a"], p["w_proj_b"], p["w_out"], p["norm_ffn"],
                      p["w_query"], MERGE_TILE)
    rows_tm, shifts_tm, shifts, gates = _topk(qp, p["sub_keys"], PEER_TILE)
    row_blocks = rows_tm.reshape(n_tok // STAGE_TOKENS, STAGE_TOKENS, PEER_SEL)
    coef_pairs = _peer_u(row_blocks, shifts_tm, gates, shifts, u, p["expert_u"], PEER_TILE)
    y = _peer_v(row_blocks, coef_pairs, h, p["norm_final"], p["expert_v"], PEER_TILE)
    return y.reshape(bn, s_len, d)


def kernel(x_prompt, x_sample, norm_mix, w_in, w_proj_a, w_proj_b, w_out, rpb, norm_ffn, w_query, sub_keys,
           expert_u, expert_v, t5_table, norm_final):
    assert norm_mix.shape[0] == 1, "single-layer model"
    p = {
        "norm_mix": norm_mix[0].reshape(1, D_MODEL).astype(F32),
        "w_in": _arrange_w_in(w_in[0]),
        "w_proj_a": w_proj_a[0].astype(BF16),
        "w_proj_b": w_proj_b[0].astype(BF16),
        "w_out": w_out[0].astype(BF16),
        "na_bias": _na_bias(rpb[0]),
        "norm_ffn": norm_ffn[0].reshape(1, D_MODEL).astype(F32),
        "w_query": w_query[0].astype(BF16),
        "sub_keys": sub_keys[0].astype(BF16),
        "expert_u": _pack_experts(expert_u[0]),
        "expert_v": _pack_experts(expert_v[0]),
        "t5_table": t5_table,
        "norm_final": norm_final.reshape(SUBLANES, LANES).astype(F32),
    }
    return (_trunk(x_prompt, p), _trunk(x_sample, p))
```

```python
import functools
import math

import numpy as np
import jax
import jax.numpy as jnp
from jax import lax
from jax.experimental import pallas as pl
from jax.experimental.pallas import tpu as pltpu

D_MODEL = 1024
HEAD_DIM = 64
DIL_GROUPS = ((128, 1), (512, 4), (2048, 16))
A_HEADS_PER_GROUP = 4
A_HEADS = A_HEADS_PER_GROUP * len(DIL_GROUPS)
A_WIDTH = A_HEADS * HEAD_DIM
A_OUT_WIDTH = A_HEADS_PER_GROUP * HEAD_DIM
T5_BUCKETS = 32
T5_MAX_DIST = 1024
GRID_W = 64
B_HEADS = 8
B_WIDTH = B_HEADS * HEAD_DIM
NB_ROWS = 8
NB_COLS = 16
PEER_HEADS = 8
PEER_NKEYS = 128
PEER_EXPERTS = PEER_NKEYS * PEER_NKEYS
PEER_QDIM = 256
PEER_HALF = PEER_QDIM // 2
PEER_TOPK = 16
PEER_SEL = PEER_HEADS * PEER_TOPK
NORM_EPS = 1e-6
NEG_INF = -1e30

RAD = 64
SUBLANES = 8
LANES = 128
HALF_EXPERTS = PEER_EXPERTS // 2
LOW_HALF = 16
DIL_TILE = 256
VMEM_LIMIT = 48 * 1024 * 1024

F32 = jnp.float32
BF16 = jnp.bfloat16


def _resident(shape):
    nd = len(shape)
    return pl.BlockSpec(shape, lambda *_: (0,) * nd, pipeline_mode=pl.Buffered(1))


def _params(sem):
    return pltpu.CompilerParams(dimension_semantics=sem, vmem_limit_bytes=VMEM_LIMIT)


IN_SPLITS = (A_WIDTH, A_WIDTH, A_WIDTH, B_WIDTH, B_WIDTH, B_WIDTH, D_MODEL, D_MODEL)
QK_SCALE = HEAD_DIM ** -0.5


def _arrange_w_in(w_in):
    qa, ka, va, qb, rest = jnp.split(w_in, [A_WIDTH, 2 * A_WIDTH, 3 * A_WIDTH, 3 * A_WIDTH + B_WIDTH], axis=1)
    groups = []
    for g in range(len(DIL_GROUPS)):
        cs = slice(g * A_OUT_WIDTH, (g + 1) * A_OUT_WIDTH)
        groups += [qa[:, cs] * QK_SCALE, ka[:, cs], va[:, cs]]
    return jnp.concatenate(groups + [qb * QK_SCALE, rest], axis=1).astype(BF16)


def _rms(x, g):
    return x * lax.rsqrt(jnp.mean(x * x, axis=-1, keepdims=True) + NORM_EPS) * g


def _inproj_kernel(x_ref, g_ref, w_ref, *out_refs):
    u = _rms(x_ref[...], g_ref[...]).astype(BF16)
    off = 0
    for o_ref, width in zip(out_refs, IN_SPLITS):
        o_ref[...] = jnp.dot(u, w_ref[:, off:off + width], preferred_element_type=F32).astype(o_ref.dtype)
        off += width


def _inproj(x2, g, w_bf16, tm):
    n_tok = x2.shape[0]
    return pl.pallas_call(
        _inproj_kernel,
        grid=(n_tok // tm,),
        in_specs=[pl.BlockSpec((tm, D_MODEL), lambda i: (i, 0)),
                  _resident((1, D_MODEL)),
                  _resident(w_bf16.shape)],
        out_specs=[pl.BlockSpec((tm, w), lambda i: (i, 0)) for w in IN_SPLITS],
        out_shape=[jax.ShapeDtypeStruct((n_tok, w), BF16) for w in IN_SPLITS],
        compiler_params=_params(("parallel",)),
        name="inproj",
    )(x2, g, w_bf16)


def _t5_bucket(rel):
    n = -rel
    nb = T5_BUCKETS // 2
    ret = (n < 0).astype(np.int32) * nb
    n = np.abs(n)
    max_exact = nb // 2
    large = max_exact + (np.log(np.maximum(n, 1) / max_exact) / math.log(T5_MAX_DIST / max_exact)
                         * (nb - max_exact)).astype(np.int32)
    large = np.minimum(large, nb - 1)
    return (ret + np.where(n < max_exact, n, large)).astype(np.int32)


def _dilated_bias(t5_table, group, dilation, tq):
    rel = np.arange(tq + 2 * RAD)[None, :] - RAD - np.arange(tq)[:, None]
    tab = t5_table[:, group * A_HEADS_PER_GROUP:(group + 1) * A_HEADS_PER_GROUP].astype(F32)
    in_band = np.abs(rel) <= RAD
    bucket = np.where(in_band, _t5_bucket(rel * dilation), -1)
    bias = jnp.full((A_HEADS_PER_GROUP,) + rel.shape, NEG_INF, F32)
    for b in np.unique(bucket[in_band]):
        bias = jnp.where(jnp.asarray(bucket == b)[None], tab[int(b)][:, None, None], bias)
    return bias


def _dilated_kernel(q_ref, kp_ref, kc_ref, kn_ref, vp_ref, vc_ref, vn_ref, bias_ref, o_ref, l_ref, *, tq, n_tiles):
    i = pl.program_id(2)
    q = q_ref[0]
    k_all = jnp.concatenate([kp_ref[0], kc_ref[0], kn_ref[0]], axis=0)
    v_all = jnp.concatenate([vp_ref[0], vc_ref[0], vn_ref[0]], axis=0)
    nk = tq + 2 * RAD
    col = lax.broadcasted_iota(jnp.int32, (tq, nk), 1)
    first_valid = jnp.where(i == 0, RAD, 0)
    end_valid = jnp.where(i == n_tiles - 1, tq + RAD, nk)
    valid = (col >= first_valid) & (col < end_valid)
    lane = lax.broadcasted_iota(jnp.int32, (tq, A_OUT_WIDTH), 1)
    heads = [(lane >= h * HEAD_DIM) & (lane < (h + 1) * HEAD_DIM) for h in range(A_HEADS_PER_GROUP)]
    scores = []
    for h, head in enumerate(heads):
        qh = jnp.where(head, q, jnp.zeros_like(q))
        s = lax.dot_general(qh, k_all, (((1,), (1,)), ((), ())), preferred_element_type=F32)
        scores.append(jnp.where(valid, s + bias_ref[h], NEG_INF))
    stats = []
    for s in scores:
        m = jnp.max(s, axis=-1, keepdims=True)
        p = jnp.exp(s - m)
        stats.append((m, p, jnp.sum(p, axis=-1, keepdims=True)))
    o_acc = jnp.zeros((tq, A_OUT_WIDTH), F32)
    l_acc = jnp.zeros((tq, A_OUT_WIDTH), F32)
    for head, (m, p, den) in zip(heads, stats):
        o = jnp.dot(p.astype(BF16), v_all, preferred_element_type=F32) / den
        o_acc = jnp.where(head, o, o_acc)
        l_acc = jnp.where(head, m + jnp.log(den), l_acc)
    o_ref[0] = o_acc
    l_ref[0] = l_acc


def _dilated_group(qkv, bias, group, dilation, tq):
    bn, s_len, _ = qkv.shape
    sub_len = s_len // dilation
    n_tiles = sub_len // tq
    blk = tq // RAD
    n_rad = sub_len // RAD
    view = qkv.reshape(bn, sub_len, dilation * A_WIDTH)

    def specs(part):
        cur = pl.BlockSpec((1, tq, A_OUT_WIDTH), lambda b, r, i: (b, i, r * 3 + part))
        prev = pl.BlockSpec((1, RAD, A_OUT_WIDTH), lambda b, r, i: (b, jnp.maximum(i * blk - 1, 0), r * 3 + part))
        nxt = pl.BlockSpec((1, RAD, A_OUT_WIDTH),
                           lambda b, r, i: (b, jnp.minimum((i + 1) * blk, n_rad - 1), r * 3 + part))
        return cur, prev, nxt

    out = pl.BlockSpec((1, tq, A_OUT_WIDTH), lambda b, r, i: (b, i, r))
    (q_cur, _, _), (k_cur, k_prev, k_next), (v_cur, v_prev, v_next) = specs(0), specs(1), specs(2)
    o, l = pl.pallas_call(
        functools.partial(_dilated_kernel, tq=tq, n_tiles=n_tiles),
        grid=(bn, dilation, n_tiles),
        in_specs=[q_cur, k_prev, k_cur, k_next, v_prev, v_cur, v_next, _resident(bias.shape)],
        out_specs=[out, out],
        out_shape=[jax.ShapeDtypeStruct((bn, sub_len, dilation * A_OUT_WIDTH), F32)] * 2,
        compiler_params=_params(("parallel", "parallel", "parallel")),
        name=f"dilated{group}",
    )(view, view, view, view, view, view, view, bias)
    return o.reshape(bn, s_len, A_OUT_WIDTH), l.reshape(bn, s_len, A_OUT_WIDTH)


NA_TILE_ROWS = 8
NA_TILE = NA_TILE_ROWS * GRID_W
NA_KEYS = NB_ROWS * GRID_W


def _na_bias(rpb):
    cols = np.arange(GRID_W)
    col_start = np.clip(cols - NB_COLS // 2, 0, GRID_W - NB_COLS)
    col_valid = (cols[None, :] >= col_start[:, None]) & (cols[None, :] < col_start[:, None] + NB_COLS)
    dc = np.clip(cols[None, :] - cols[:, None] + NB_COLS - 1, 0, 2 * NB_COLS - 2)
    onehot = jnp.asarray(dc.reshape(-1)[:, None] == np.arange(2 * NB_COLS - 1)[None, :], F32)
    by_col = jnp.einsum("hab,pb->hap", rpb.astype(F32), onehot, precision=lax.Precision.HIGHEST)
    by_col = by_col.reshape(B_HEADS, 2 * NB_ROWS - 1, GRID_W, GRID_W)
    variants = []
    for v in range(NB_ROWS):
        tab = by_col[:, v:v + NB_ROWS]
        tab = jnp.where(jnp.asarray(col_valid)[None, None], tab, NEG_INF)
        variants.append(jnp.transpose(tab, (0, 2, 1, 3)).reshape(B_HEADS // 2, 2 * GRID_W, NA_KEYS))
    return jnp.stack(variants, axis=0)


def _na_kernel(q_ref, kp_ref, kc_ref, kn_ref, vp_ref, vc_ref, vn_ref, bias_ref, o_ref, kwin, vwin, *, rows):
    i = pl.program_id(1)
    kwin[0:NA_TILE] = kp_ref[0]
    kwin[NA_TILE:2 * NA_TILE] = kc_ref[0]
    kwin[2 * NA_TILE:3 * NA_TILE] = kn_ref[0]
    vwin[0:NA_TILE] = vp_ref[0]
    vwin[NA_TILE:2 * NA_TILE] = vc_ref[0]
    vwin[2 * NA_TILE:3 * NA_TILE] = vn_ref[0]
    pair = 2 * HEAD_DIM
    row_id = lax.broadcasted_iota(jnp.int32, (2 * GRID_W, pair), 0)
    lane_id = lax.broadcasted_iota(jnp.int32, (2 * GRID_W, pair), 1)
    own = (row_id < GRID_W) == (lane_id < HEAD_DIM)
    low = lax.broadcasted_iota(jnp.int32, (GRID_W, pair), 1) < HEAD_DIM

    def query_row(a, carry):
        r = i * NA_TILE_ROWS + a
        rs = jnp.clip(r - NB_ROWS // 2, 0, rows - NB_ROWS)
        start = pl.multiple_of((rs - (i - 1) * NA_TILE_ROWS) * GRID_W, GRID_W)
        variant = rs - r + NB_ROWS - 1
        q_rows = pl.ds(pl.multiple_of(a * GRID_W, GRID_W), GRID_W)
        scores = []
        for hp in range(B_HEADS // 2):
            cs = slice(hp * pair, (hp + 1) * pair)
            q2 = q_ref[0, q_rows, cs]
            qs = jnp.where(own, jnp.concatenate([q2, q2], axis=0), jnp.zeros((2 * GRID_W, pair), q2.dtype))
            s = lax.dot_general(qs, kwin[pl.ds(start, NA_KEYS), cs], (((1,), (1,)), ((), ())),
                                preferred_element_type=F32)
            scores.append(s + bias_ref[variant, hp])
        probs = []
        for s in scores:
            p = jnp.exp(s - jnp.max(s, axis=-1, keepdims=True))
            probs.append((p, jnp.sum(p, axis=-1, keepdims=True)))
        outs = []
        for hp, (p, den) in enumerate(probs):
            cs = slice(hp * pair, (hp + 1) * pair)
            o = jnp.dot(p.astype(BF16), vwin[pl.ds(start, NA_KEYS), cs], preferred_element_type=F32) / den
            outs.append(jnp.where(low, o[0:GRID_W], o[GRID_W:]))
        o_ref[0, q_rows, :] = jnp.concatenate(outs, axis=1).astype(o_ref.dtype)
        return carry

    lax.fori_loop(0, NA_TILE_ROWS, query_row, 0)


def _neighbourhood(qb, kb, vb, bias):
    bn, s_len, _ = qb.shape
    rows = s_len // GRID_W
    n_tiles = rows // NA_TILE_ROWS
    cur = pl.BlockSpec((1, NA_TILE, B_WIDTH), lambda b, i: (b, i, 0))
    prev = pl.BlockSpec((1, NA_TILE, B_WIDTH), lambda b, i: (b, jnp.maximum(i - 1, 0), 0))
    nxt = pl.BlockSpec((1, NA_TILE, B_WIDTH), lambda b, i: (b, jnp.minimum(i + 1, n_tiles - 1), 0))
    return pl.pallas_call(
        functools.partial(_na_kernel, rows=rows),
        grid=(bn, n_tiles),
        in_specs=[cur, prev, cur, nxt, prev, cur, nxt, _resident(bias.shape)],
        out_specs=cur,
        out_shape=jax.ShapeDtypeStruct((bn, s_len, B_WIDTH), BF16),
        scratch_shapes=[pltpu.VMEM((3 * NA_TILE, B_WIDTH), BF16)] * 2,
        compiler_params=_params(("parallel", "parallel")),
        name="neighbourhood",
    )(qb, kb, kb, kb, vb, vb, vb, bias)


def _merge_kernel(x_ref, o1_ref, l1_ref, o2_ref, l2_ref, o3_ref, l3_ref, ob_ref, ga_ref, gb_ref,
                  wpa_ref, wpb_ref, wout_ref, gffn_ref, wq_ref, h_ref, u_ref, qp_ref):
    l1, l2, l3 = l1_ref[...], l2_ref[...], l3_ref[...]
    m = jnp.maximum(jnp.maximum(l1, l2), l3)
    w1, w2, w3 = jnp.exp(l1 - m), jnp.exp(l2 - m), jnp.exp(l3 - m)
    oa = (w1 * o1_ref[...] + w2 * o2_ref[...] + w3 * o3_ref[...]) / (w1 + w2 + w3)
    pa = jnp.dot(oa.astype(BF16), wpa_ref[...], preferred_element_type=F32)
    pb = jnp.dot(ob_ref[...], wpb_ref[...], preferred_element_type=F32)
    merged = jax.nn.sigmoid(ga_ref[...].astype(F32)) * pa + jax.nn.sigmoid(gb_ref[...].astype(F32)) * pb
    h = x_ref[...] + jnp.dot(merged.astype(BF16), wout_ref[...], preferred_element_type=F32)
    u = _rms(h, gffn_ref[...])
    for s in range(SUBLANES):
        h_ref[:, s, :] = h[:, s * LANES:(s + 1) * LANES]
        u_ref[:, s, :] = u[:, s * LANES:(s + 1) * LANES]
    qp_ref[...] = jnp.dot(u.astype(BF16), wq_ref[...], preferred_element_type=F32).astype(qp_ref.dtype)


def _merge(x2, dil_outs, ob, ga, gb, wpa, wpb, wout, gffn, wq, tm):
    n_tok = x2.shape[0]
    tok = lambda w: pl.BlockSpec((tm, w), lambda i: (i, 0))
    ins = [x2]
    specs = [tok(D_MODEL)]
    for o, l in dil_outs:
        ins += [o, l]
        specs += [tok(A_OUT_WIDTH), tok(A_OUT_WIDTH)]
    ins += [ob, ga, gb, wpa, wpb, wout, gffn, wq]
    specs += [tok(B_WIDTH), tok(D_MODEL), tok(D_MODEL), _resident(wpa.shape), _resident(wpb.shape),
              _resident(wout.shape), _resident(gffn.shape), _resident(wq.shape)]
    qw = PEER_HEADS * PEER_QDIM
    tile = pl.BlockSpec((tm, SUBLANES, LANES), lambda i: (i, 0, 0))
    return pl.pallas_call(
        _merge_kernel,
        grid=(n_tok // tm,),
        in_specs=specs,
        out_specs=[tile, tile, tok(qw)],
        out_shape=[jax.ShapeDtypeStruct((n_tok, SUBLANES, LANES), F32)] * 2 + [jax.ShapeDtypeStruct((n_tok, qw), BF16)],
        compiler_params=_params(("parallel",)),
        name="merge",
    )(*ins)


def _take_top(problems, count):
    state = [s for s, _, _, _ in problems]
    for j in range(count):
        for i, (_, val_ref, pick_ref, payload) in enumerate(problems):
            s = state[i]
            iota = lax.broadcasted_iota(jnp.int32, s.shape, 0).astype(F32)
            m = jnp.max(s, axis=0, keepdims=True)
            pos = jnp.min(jnp.where(s == m, iota, float(s.shape[0])), axis=0, keepdims=True)
            hit = iota == pos
            val_ref[j:j + 1, :] = m
            pick_ref[j:j + 1, :] = (pos if payload is None
                                    else jnp.max(jnp.where(hit, payload, -1.0), axis=0, keepdims=True))
            state[i] = jnp.where(hit, -jnp.inf, s)


PAIR_COUNTS = tuple(PEER_TOPK // (a + 1) for a in range(PEER_TOPK))
N_PAIRS = sum(PAIR_COUNTS)
PAIR_ROWS = -(-N_PAIRS // SUBLANES) * SUBLANES


TOPK_HEADS_PER_STEP = 8


def _topk_kernel(qp_ref, sk_ref, row_tm_ref, sh_tm_ref, sh_ref, gate_ref,
                 row_scr, sh_scr, va_ref, ia_ref, vb_ref, ib_ref, cand_ref, cidx_ref, best_ref, pick_ref):
    tm = qp_ref.shape[0]
    step = pl.program_id(1)
    stage1 = []
    for hh in range(TOPK_HEADS_PER_STEP):
        for p, (val_ref, idx_ref) in enumerate(((va_ref, ia_ref), (vb_ref, ib_ref))):
            c0 = hh * PEER_QDIM + p * PEER_HALF
            q = qp_ref[:, c0:c0 + PEER_HALF]
            s = lax.dot_general(sk_ref[p], q, (((1,), (1,)), ((), ())), preferred_element_type=F32)
            stage1.append((s, val_ref.at[hh], idx_ref.at[hh], None))
    _take_top(stage1, PEER_TOPK)
    stage2 = []
    for hh in range(TOPK_HEADS_PER_STEP):
        cand_ref[hh, N_PAIRS:, :] = jnp.full((PAIR_ROWS - N_PAIRS, tm), -jnp.inf, F32)
        cidx_ref[hh, N_PAIRS:, :] = jnp.zeros((PAIR_ROWS - N_PAIRS, tm), F32)
        at = 0
        for a, count in enumerate(PAIR_COUNTS):
            cand_ref[hh, at:at + count, :] = va_ref[hh, a:a + 1, :] + vb_ref[hh, 0:count, :]
            cidx_ref[hh, at:at + count, :] = ia_ref[hh, a:a + 1, :] * PEER_NKEYS + ib_ref[hh, 0:count, :]
            at += count
        stage2.append((cand_ref[hh], best_ref.at[hh], pick_ref.at[hh], cidx_ref[hh]))
    _take_top(stage2, PEER_TOPK)
    rows_per_step = TOPK_HEADS_PER_STEP * PEER_TOPK
    best = best_ref[...]
    e = jnp.exp(best - best[:, 0:1])
    gate_ref[0] = (e / jnp.sum(e, axis=1, keepdims=True)).reshape(rows_per_step, tm)
    experts = pick_ref[...].astype(jnp.int32).reshape(rows_per_step, tm)
    shift = jnp.where(experts < HALF_EXPERTS, LOW_HALF, 0)
    sh_ref[0] = shift
    slots = pl.ds(pl.multiple_of(step * rows_per_step, rows_per_step), rows_per_step)
    row_scr[slots, :] = (experts & (HALF_EXPERTS - 1)) * SUBLANES
    sh_scr[slots, :] = shift

    @pl.when(step == PEER_HEADS // TOPK_HEADS_PER_STEP - 1)
    def _():
        row_tm_ref[0] = row_scr[...].T
        sh_tm_ref[0] = sh_scr[...].T


def _topk(qp, sk_bf16, tm):
    n_tok = qp.shape[0]
    n_tiles = n_tok // tm
    hps = TOPK_HEADS_PER_STEP
    per_step = pl.BlockSpec((1, hps * PEER_TOPK, tm), lambda i, h: (i, h, 0))
    per_tile = pl.BlockSpec((1, tm, PEER_SEL), lambda i, h: (i, 0, 0))
    slot_major = (n_tiles, PEER_SEL, tm)
    token_major = (n_tiles, tm, PEER_SEL)
    return pl.pallas_call(
        _topk_kernel,
        grid=(n_tiles, PEER_HEADS // hps),
        in_specs=[pl.BlockSpec((tm, hps * PEER_QDIM), lambda i, h: (i, h)), _resident(sk_bf16.shape)],
        out_specs=[per_tile, per_tile, per_step, per_step],
        out_shape=[jax.ShapeDtypeStruct(token_major, jnp.int32), jax.ShapeDtypeStruct(token_major, jnp.int32),
                   jax.ShapeDtypeStruct(slot_major, jnp.int32), jax.ShapeDtypeStruct(slot_major, F32)],
        scratch_shapes=[pltpu.VMEM((PEER_SEL, tm), jnp.int32)] * 2
        + [pltpu.VMEM((hps, PEER_TOPK, tm), F32)] * 4 + [pltpu.VMEM((hps, PAIR_ROWS, tm), F32)] * 2
        + [pltpu.VMEM((hps, PEER_TOPK, tm), F32)] * 2,
        compiler_params=_params(("parallel", "arbitrary")),
        name="peer_topk",
    )(qp, sk_bf16)


def _pack_experts(tbl):
    b = lax.bitcast_convert_type(tbl.astype(BF16), jnp.uint16).astype(jnp.uint32)
    w = (b[HALF_EXPERTS:] << 16) | b[:HALF_EXPERTS]
    return lax.bitcast_convert_type(w, jnp.int32).reshape(HALF_EXPERTS * SUBLANES, LANES)


def _table_tile(tbl_ref, offset):
    return tbl_ref[pl.ds(pl.multiple_of(offset, SUBLANES), SUBLANES), :]


PACKED_ROWS = 2 * SUBLANES
V_CHUNK = 32
V_CHUNK_COLS = V_CHUNK * PACKED_ROWS
V_COLS = PEER_SEL * PACKED_ROWS


def _expand_matrix():
    col = np.arange(V_COLS)
    src = np.arange(2 * PEER_SEL)
    e = (src[:, None] % PEER_SEL == col[None, :] // PACKED_ROWS) & (src[:, None] // PEER_SEL == col[None, :] % 2)
    return jnp.asarray(e, BF16)


def _chunk_left(lhs, c):
    c0 = c * V_CHUNK_COLS
    half = V_CHUNK_COLS // 2
    return jnp.concatenate([lhs[:, c0:c0 + half], lhs[:, c0 + half:c0 + V_CHUNK_COLS]], axis=0)


def _chunk_weights(tiles):
    return jnp.concatenate([jnp.concatenate(tiles[:V_CHUNK // 2], axis=0),
                            jnp.concatenate(tiles[V_CHUNK // 2:], axis=0)], axis=1)


def _peer_u_kernel(rows_hbm, x_ref, gate_ref, sh_ref, sh_tm_ref, e_ref, d_ref, tbl_ref, pair_ref,
                   sums_ref, fe_ref, stage_sm, sems):
    tm = x_ref.shape[0]
    lane = lax.broadcasted_iota(jnp.int32, (PEER_SEL, tm), 1)
    low_tm = sh_tm_ref[0] == LOW_HALF
    flags = jnp.concatenate([low_tm, jnp.logical_not(low_tm)], axis=1)
    fe_ref[...] = jnp.dot(jnp.where(flags, 1.0, 0.0).astype(BF16), e_ref[...], preferred_element_type=F32)
    blockdiag = d_ref[...]
    half_sub = V_CHUNK // 2

    def token(t, offset):
        xb = lax.bitcast_convert_type(x_ref[t].astype(BF16).astype(F32), jnp.int32)
        x2 = pltpu.bitcast(xb | lax.shift_right_logical(xb, 16), BF16)
        lhs = (fe_ref[pl.ds(t, 1), :] * blockdiag).astype(BF16)
        for c in range(PEER_SEL // V_CHUNK):
            tiles = [pltpu.bitcast(_table_tile(tbl_ref, offset(c * V_CHUNK + j)), BF16) * x2
                     for j in range(V_CHUNK)]
            out = jnp.dot(_chunk_left(lhs, c), _chunk_weights(tiles), preferred_element_type=F32)
            at = pl.multiple_of(t * PEER_SEL + c * V_CHUNK, V_CHUNK)
            sums_ref[pl.ds(at, half_sub), :] = out[0:half_sub, 0:LANES]
            sums_ref[pl.ds(at + half_sub, half_sub), :] = out[half_sub:, LANES:]

    _for_staged_tokens(rows_hbm, stage_sm, sems, tm, token)

    def finish(i, act):
        for j in range(SUBLANES):
            t = i * SUBLANES + j
            rows = sums_ref[pl.ds(pl.multiple_of(t * PEER_SEL, PEER_SEL), PEER_SEL), :]
            dots = jnp.sum(rows, axis=-1, keepdims=True)
            act = jnp.where(lane == t, dots, act)
        return act

    act = lax.fori_loop(0, tm // SUBLANES, finish, jnp.zeros((PEER_SEL, tm), F32))
    coef = gate_ref[0] * jax.nn.gelu(act)
    low = sh_ref[0] == LOW_HALF
    pair_ref[:, 0:PEER_SEL] = jnp.where(low, coef, 0.0).T
    pair_ref[:, PEER_SEL:] = jnp.where(low, 0.0, coef).T


STAGE_TOKENS = 4 * SUBLANES
STAGE_SLOTS = 2


def _for_staged_tokens(rows_hbm, stage_sm, sems, tm, token_fn):
    tile = pl.program_id(0)
    n_blocks = tm // STAGE_TOKENS
    total = pl.num_programs(0) * n_blocks

    def copy(block, slot):
        return pltpu.make_async_copy(rows_hbm.at[block], stage_sm.at[slot], sems.at[slot])

    @pl.when(tile == 0)
    def _():
        for slot in range(STAGE_SLOTS):
            copy(slot, slot).start()

    def round_(i, carry):
        for slot in range(STAGE_SLOTS):
            local = i * STAGE_SLOTS + slot
            block = tile * n_blocks + local
            copy(block, slot).wait()
            for j in range(STAGE_TOKENS):
                token_fn(local * STAGE_TOKENS + j, lambda k, slot=slot, j=j: stage_sm[slot, j, k])
            copy(jnp.minimum(block + STAGE_SLOTS, total - 1), slot).start()
        return carry

    lax.fori_loop(0, n_blocks // STAGE_SLOTS, round_, 0)

    @pl.when(tile == pl.num_programs(0) - 1)
    def _():
        for slot in range(STAGE_SLOTS):
            copy(total - 1, slot).wait()


def _peer_v_kernel(rows_hbm, c2_ref, h_ref, g_ref, e_ref, d_ref, tbl_ref, y_ref, ce_ref, acc_ref, stage_sm, sems):
    tm = h_ref.shape[0]
    c2 = c2_ref[...]
    c_hi = c2.astype(BF16)
    c_lo = (c2 - c_hi.astype(F32)).astype(BF16)
    ce_ref[0] = jnp.dot(c_hi, e_ref[...], preferred_element_type=F32)
    ce_ref[1] = jnp.dot(c_lo, e_ref[...], preferred_element_type=F32)
    diag = d_ref[...]

    def token(t, offset):
        lhs = jnp.concatenate([ce_ref[0, pl.ds(t, 1), :] * diag, ce_ref[1, pl.ds(t, 1), :] * diag],
                              axis=0).astype(BF16)
        acc = None
        for c in range(PEER_SEL // V_CHUNK):
            tiles = [pltpu.bitcast(_table_tile(tbl_ref, offset(c * V_CHUNK + j)), BF16) for j in range(V_CHUNK)]
            out = jnp.dot(_chunk_left(lhs, c), _chunk_weights(tiles), preferred_element_type=F32)
            acc = out if acc is None else acc + out
        first = acc[0:SUBLANES, 0:LANES] + acc[SUBLANES:2 * SUBLANES, 0:LANES]
        second = acc[2 * SUBLANES:3 * SUBLANES, LANES:] + acc[3 * SUBLANES:, LANES:]
        acc_ref[t] = h_ref[t] + (first + second)

    _for_staged_tokens(rows_hbm, stage_sm, sems, tm, token)
    h = acc_ref[...]
    ms = jnp.sum(jnp.sum(h * h, axis=2, keepdims=True), axis=1, keepdims=True) * (1.0 / D_MODEL)
    acc_ref[...] = h * lax.rsqrt(ms + NORM_EPS) * g_ref[...]
    for s in range(SUBLANES):
        y_ref[:, s * LANES:(s + 1) * LANES] = acc_ref[:, s, :]


def _peer_u(rows_tm, shifts_tm, gates, shifts, u3, tbl, tm):
    n_tiles = gates.shape[0]
    vm = pl.BlockSpec((1, PEER_SEL, tm), lambda i: (i, 0, 0))
    expand = _expand_matrix()
    col = np.arange(V_COLS)
    blockdiag = jnp.asarray((col[None, :] // PACKED_ROWS) % (V_CHUNK // 2) == np.arange(V_CHUNK // 2)[:, None], F32)
    return pl.pallas_call(
        _peer_u_kernel,
        grid=(n_tiles,),
        in_specs=[pl.BlockSpec(memory_space=pl.ANY), pl.BlockSpec((tm, SUBLANES, LANES), lambda i: (i, 0, 0)), vm, vm,
                  pl.BlockSpec((1, tm, PEER_SEL), lambda i: (i, 0, 0)),
                  _resident(expand.shape), _resident(blockdiag.shape), _resident(tbl.shape)],
        out_specs=pl.BlockSpec((tm, 2 * PEER_SEL), lambda i: (i, 0)),
        out_shape=jax.ShapeDtypeStruct((n_tiles * tm, 2 * PEER_SEL), F32),
        scratch_shapes=[pltpu.VMEM((tm * PEER_SEL, LANES), F32), pltpu.VMEM((tm, V_COLS), F32),
                        pltpu.SMEM((STAGE_SLOTS, STAGE_TOKENS, PEER_SEL), jnp.int32),
                        pltpu.SemaphoreType.DMA((STAGE_SLOTS,))],
        compiler_params=_params(("arbitrary",)),
        name="peer_u",
    )(rows_tm, u3, gates, shifts, shifts_tm, expand, blockdiag, tbl)


def _peer_v(rows, coef_pairs, h3, g_final, tbl, tm):
    n_tiles = h3.shape[0] // tm
    tok = pl.BlockSpec((tm, SUBLANES, LANES), lambda i: (i, 0, 0))
    expand = _expand_matrix()
    diag = jnp.asarray((np.arange(V_COLS)[None, :] % PACKED_ROWS) // 2 == np.arange(SUBLANES)[:, None], F32)
    return pl.pallas_call(
        _peer_v_kernel,
        grid=(n_tiles,),
        in_specs=[pl.BlockSpec(memory_space=pl.ANY), pl.BlockSpec((tm, 2 * PEER_SEL), lambda i: (i, 0)), tok,
                  _resident(g_final.shape), _resident(expand.shape), _resident(diag.shape), _resident(tbl.shape)],
        out_specs=pl.BlockSpec((tm, D_MODEL), lambda i: (i, 0)),
        out_shape=jax.ShapeDtypeStruct((h3.shape[0], D_MODEL), F32),
        scratch_shapes=[pltpu.VMEM((2, tm, V_COLS), F32), pltpu.VMEM((tm, SUBLANES, LANES), F32),
                        pltpu.SMEM((STAGE_SLOTS, STAGE_TOKENS, PEER_SEL), jnp.int32),
                        pltpu.SemaphoreType.DMA((STAGE_SLOTS,))],
        compiler_params=_params(("arbitrary",)),
        name="peer_v",
    )(rows, coef_pairs, h3, g_final, expand, diag, tbl)


PROJ_TILE = 512
MERGE_TILE = 256
PEER_TILE = 128


def _trunk(x, p):
    bn, s_len, d = x.shape
    n_tok = bn * s_len
    x2 = x.reshape(n_tok, d)
    *qkv_groups, qb, kb, vb, ga, gb = _inproj(x2, p["norm_mix"], p["w_in"], PROJ_TILE)
    seq = lambda t: t.reshape(bn, s_len, t.shape[-1])
    dil_outs = []
    for group, (window, dilation) in enumerate(DIL_GROUPS):
        assert window // (2 * dilation) == RAD
        sub_len = s_len // dilation
        assert s_len % dilation == 0 and sub_len % RAD == 0
        tq = min(DIL_TILE, sub_len)
        assert sub_len % tq == 0
        bias = _dilated_bias(p["t5_table"], group, dilation, tq)
        o, l = _dilated_group(seq(qkv_groups[group]), bias, group, dilation, tq)
        dil_outs.append((o.reshape(n_tok, A_OUT_WIDTH), l.reshape(n_tok, A_OUT_WIDTH)))
    assert s_len % NA_TILE == 0 and s_len // GRID_W >= NB_ROWS
    ob = _neighbourhood(seq(qb), seq(kb), seq(vb), p["na_bias"]).reshape(n_tok, B_WIDTH)
    h, u, qp = _merge(x2, dil_outs, ob, ga, gb, p["w_proj_a"], p["w_proj_b"], p["w_out"], p["norm_ffn"],
                      p["w_query"], MERGE_TILE)
    rows_tm, shifts_tm, shifts, gates = _topk(qp, p["sub_keys"], PEER_TILE)
    row_blocks = rows_tm.reshape(n_tok // STAGE_TOKENS, STAGE_TOKENS, PEER_SEL)
    coef_pairs = _peer_u(row_blocks, shifts_tm, gates, shifts, u, p["expert_u"], PEER_TILE)
    y = _peer_v(row_blocks, coef_pairs, h, p["norm_final"], p["expert_v"], PEER_TILE)
    return y.reshape(bn, s_len, d)


def kernel(x_prompt, x_sample, norm_mix, w_in, w_proj_a, w_proj_b, w_out, rpb, norm_ffn, w_query, sub_keys,
           expert_u, expert_v, t5_table, norm_final):
    assert norm_mix.shape[0] == 1, "single-layer model"
    p = {
        "norm_mix": norm_mix[0].reshape(1, D_MODEL).astype(F32),
        "w_in": _arrange_w_in(w_in[0]),
        "w_proj_a": w_proj_a[0].astype(BF16),
        "w_proj_b": w_proj_b[0].astype(BF16),
        "w_out": w_out[0].astype(BF16),
        "na_bias": _na_bias(rpb[0]),
        "norm_ffn": norm_ffn[0].reshape(1, D_MODEL).astype(F32),
        "w_query": w_query[0].astype(BF16),
        "sub_keys": sub_keys[0].astype(BF16),
        "expert_u": _pack_experts(expert_u[0]),
        "expert_v": _pack_experts(expert_v[0]),
        "t5_table": t5_table,
        "norm_final": norm_final.reshape(SUBLANES, LANES).astype(F32),
    }
    return (_trunk(x_prompt, p), _trunk(x_sample, p))
```
